```python
import jax, jax.numpy as jnp
from jax import lax
import numpy as np

D_MODEL = 1024
BATCH = 2
SEQ = 8192
DEPTH = 2

LRU_WIDTH = D_MODEL
LRU_BLOCKS = 16
LRU_BW = LRU_WIDTH // LRU_BLOCKS
CONV_W = 4
LRU_C = 8.0
DIL_PATTERNS = ((128, 1), (512, 4), (2048, 16))
N_GROUPS = 3
HEADS_PER_GROUP = 8
HEAD_DIM = 64
ATT_WIDTH = N_GROUPS * HEADS_PER_GROUP * HEAD_DIM
ATT_OUT = HEADS_PER_GROUP * HEAD_DIM
BAND_BLK = 128
ROPE_THETA = 10000.0
GLA_HEADS = 4
GLA_DK = D_MODEL // 2 // GLA_HEADS
GLA_DV = D_MODEL // GLA_HEADS
GLA_KW = GLA_HEADS * GLA_DK
GLA_VW = GLA_HEADS * GLA_DV
GLA_RANK = 16
GLA_NORMALIZER = 16.0
GLA_CHUNK = 64
N_BRANCH = 3
EPS = 1e-6
IN_SIZES = (LRU_WIDTH, LRU_WIDTH, ATT_WIDTH, ATT_WIDTH, ATT_WIDTH, ATT_OUT,
            GLA_KW, GLA_KW, GLA_VW, GLA_VW, N_BRANCH * D_MODEL)
N_IN = sum(IN_SIZES)

kernel_name = 'hybrid_rglru_dilatedswa_gla_parallel_gated'


def rmsnorm(x, g):
    xf = x.astype(jnp.float32)
    y = xf * lax.rsqrt(jnp.mean(xf * xf, axis=-1, keepdims=True) + EPS)
    return (y * g.astype(jnp.float32)).astype(x.dtype)


def rope(x, pos):
    half = x.shape[-1] // 2
    inv = ROPE_THETA ** (-(jnp.arange(half, dtype=jnp.float32) / half))
    ang = pos.astype(jnp.float32)[:, :, None] * inv
    ang = ang.reshape(ang.shape[:2] + (1,) * (x.ndim - 3) + (half,))
    cos, sin = jnp.cos(ang), jnp.sin(ang)
    xf = x.astype(jnp.float32)
    x1, x2 = xf[..., :half], xf[..., half:]
    return jnp.concatenate([x1 * cos - x2 * sin, x2 * cos + x1 * sin], axis=-1).astype(x.dtype)


def causal_dwconv(x, w, b):
    K = w.shape[0]
    S = x.shape[1]
    xp = jnp.pad(x, ((0, 0), (K - 1, 0), (0, 0)))
    y = b + xp[:, 0:S] * w[0]
    for k in range(1, K):
        y = y + xp[:, k:k + S] * w[k]
    return y


def rg_lru(x, w_a, b_a, w_x, b_x, lam):
    B, S, W = x.shape
    xf = x.astype(jnp.float32)
    xb = xf.reshape(B, S, LRU_BLOCKS, LRU_BW)
    r = jax.nn.sigmoid(jnp.einsum('bsnj,njk->bsnk', xb, w_a.astype(jnp.float32)).reshape(B, S, W) + b_a)
    i = jax.nn.sigmoid(jnp.einsum('bsnj,njk->bsnk', xb, w_x.astype(jnp.float32)).reshape(B, S, W) + b_x)
    log_a = -LRU_C * r * jax.nn.softplus(-lam.astype(jnp.float32))
    a = jnp.exp(log_a)
    u = jnp.sqrt(-jnp.expm1(2.0 * log_a)) * (i * xf)

    def combine(e1, e2):
        a1, b1 = e1
        a2, b2 = e2
        return a1 * a2, a2 * b1 + b2

    _, h = lax.associative_scan(combine, (a, u), axis=1)
    return h.astype(x.dtype)


def dilated_group(q, k, v, window, dil):
    B, S, H, dh = q.shape
    span = window // dil
    unit = dil * BAND_BLK
    pad = (-S) % unit
    L = (S + pad) // dil
    nb = L // BAND_BLK

    def to_blocks(t):
        t = jnp.pad(t, ((0, 0), (0, pad), (0, 0), (0, 0)))
        t = t.reshape(B, L, dil, H, dh).transpose(0, 2, 1, 3, 4)
        return t.reshape(B, dil, nb, BAND_BLK, H, dh)

    def band_keys(t):
        prev = jnp.pad(t, ((0, 0), (0, 0), (1, 0), (0, 0), (0, 0), (0, 0)))[:, :, :-1]
        return jnp.concatenate([prev, t], axis=3)

    qb = to_blocks(q)
    kc = band_keys(to_blocks(k))
    vc = band_keys(to_blocks(v)).astype(jnp.float32)
    s = jnp.einsum('bdnqhe,bdnkhe->bdnhqk', qb, kc, preferred_element_type=jnp.float32)
    qi = jnp.arange(BAND_BLK)[:, None]
    kj = jnp.arange(2 * BAND_BLK)[None, :]
    rel = qi + BAND_BLK - kj
    band = (rel >= 0) & (rel <= span)
    has_prev = (jnp.arange(nb) > 0)[:, None, None] | (kj >= BAND_BLK)[None]
    mask = band[None] & has_prev
    s = jnp.where(mask[:, None], s, -jnp.inf)
    m = jnp.max(s, axis=-1, keepdims=True)
    p = jnp.exp(s - m)
    l = jnp.sum(p, axis=-1, keepdims=True)
    o = jnp.einsum('bdnhqk,bdnkhe->bdnqhe', p / l, vc)
    lse = jnp.swapaxes((m + jnp.log(l))[..., 0], 3, 4)
    o = o.reshape(B, dil, L, H, dh).transpose(0, 2, 1, 3, 4).reshape(B, L * dil, H, dh)[:, :S]
    lse = lse.reshape(B, dil, L, H).transpose(0, 2, 1, 3).reshape(B, L * dil, H)[:, :S]
    return o, lse


def gla_chunked(q, k, v, log_alpha):
    B, S, H, dk = q.shape
    dv = v.shape[-1]
    C = GLA_CHUNK
    n = S // C
    f32 = jnp.float32
    q = q.astype(f32).reshape(B, n, C, H, dk) * (dk ** -0.5)
    k = k.astype(f32).reshape(B, n, C, H, dk)
    v = v.astype(f32).reshape(B, n, C, H, dv)
    b = jnp.cumsum(log_alpha.astype(f32).reshape(B, n, C, H, dk), axis=2)
    b_last = b[:, :, -1:]
    q_dec = q * jnp.exp(b)
    k_inv = k * jnp.exp(-b)
    k_end = k * jnp.exp(b_last - b)
    causal = jnp.tril(jnp.ones((C, C), dtype=bool))
    att = jnp.where(causal, jnp.einsum('bnqhk,bnshk->bnhqs', q_dec, k_inv), 0.0)
    o = jnp.einsum('bnhqs,bnshv->bnqhv', att, v)
    kv = jnp.einsum('bnshk,bnshv->nbhkv', k_end, v)
    decay = jnp.moveaxis(jnp.exp(b_last[:, :, 0]), 1, 0)

    def step(state, inp):
        dec, kv_c = inp
        return dec[..., None] * state + kv_c, state

    _, prev = lax.scan(step, jnp.zeros((B, H, dk, dv), f32), (decay, kv))
    o = o + jnp.einsum('bnqhk,nbhkv->bnqhv', q_dec, prev)
    return o.reshape(B, S, H, dv)


def hybrid_layer(x, c_act, pos, ada_w, ada_b, norm_g, w_in, conv_w, conv_b,
                 lru_wa, lru_ba, lru_wx, lru_bx, lru_lambda, qn_g, kn_g,
                 gla_a1, gla_a2, gla_ab, gla_on_g, proj_a, proj_b, proj_c, w_o):
    B, S, D = x.shape
    mod = c_act @ ada_w + ada_b
    shift, scale, gate = jnp.split(mod, 3, axis=-1)
    h = rmsnorm(x, norm_g) * (1.0 + scale[:, None]) + shift[:, None]
    z = h @ w_in
    (a_x, a_g, b_q, b_k, b_v, b_g, c_q, c_k, c_v, c_g, m_g) = jnp.split(
        z, np.cumsum(IN_SIZES)[:-1].tolist(), axis=-1)

    ya = rg_lru(causal_dwconv(a_x, conv_w, conv_b), lru_wa, lru_ba, lru_wx, lru_bx, lru_lambda)
    ya = ya * jax.nn.silu(a_g)

    shp = (B, S, N_GROUPS, HEADS_PER_GROUP, HEAD_DIM)
    q = rope(rmsnorm(b_q.reshape(shp), qn_g[:, None, :]), pos) * (HEAD_DIM ** -0.5)
    k = rope(rmsnorm(b_k.reshape(shp), kn_g[:, None, :]), pos)
    v = b_v.reshape(shp)
    outs, lses = [], []
    for g, (win, dil) in enumerate(DIL_PATTERNS):
        o_g, lse_g = dilated_group(q[:, :, g], k[:, :, g], v[:, :, g], win, dil)
        outs.append(o_g)
        lses.append(lse_g)
    wts = jax.nn.softmax(jnp.stack(lses, axis=0), axis=0)
    yb = jnp.sum(wts[..., None] * jnp.stack(outs, axis=0), axis=0)
    yb = yb.reshape(B, S, ATT_OUT).astype(x.dtype) * jax.nn.silu(b_g)

    log_alpha = jax.nn.log_sigmoid(((h @ gla_a1) @ gla_a2 + gla_ab).astype(jnp.float32)) / GLA_NORMALIZER
    yc = gla_chunked(c_q.reshape(B, S, GLA_HEADS, GLA_DK), c_k.reshape(B, S, GLA_HEADS, GLA_DK),
                     c_v.reshape(B, S, GLA_HEADS, GLA_DV), log_alpha.reshape(B, S, GLA_HEADS, GLA_DK))
    yc = rmsnorm(yc, gla_on_g).reshape(B, S, GLA_VW).astype(x.dtype) * jax.nn.silu(c_g)

    mg = jax.nn.sigmoid(m_g).reshape(B, S, N_BRANCH, D)
    merged = mg[:, :, 0] * (ya @ proj_a) + mg[:, :, 1] * (yb @ proj_b) + mg[:, :, 2] * (yc @ proj_c)
    return x + gate[:, None] * (merged @ w_o)


def setup_inputs(seed: int = 0) -> dict:
    key = jax.random.key(seed)
    ks = jax.random.split(key, 24)
    D = D_MODEL
    nrm = lambda k, shp, s: jax.random.normal(k, shp, jnp.float32) * s
    x = nrm(ks[0], (BATCH, SEQ, D), 1.0)
    c = nrm(ks[1], (BATCH, D), 1.0)
    positions = (jnp.arange(SEQ, dtype=jnp.int32)[None, :]
                 + jax.random.randint(ks[2], (BATCH, 1), 0, 4096, dtype=jnp.int32))
    ada_w = nrm(ks[3], (DEPTH, D, 3 * D), 0.2 * D ** -0.5)
    ada_b = jnp.concatenate([nrm(ks[4], (DEPTH, 2 * D), 0.02),
                             1.0 + nrm(ks[5], (DEPTH, D), 0.02)], axis=-1)
    norm_g = 1.0 + nrm(ks[6], (DEPTH, D), 0.02)
    w_in = nrm(ks[7], (DEPTH, D, N_IN), D ** -0.5)
    conv_w = nrm(ks[8], (DEPTH, CONV_W, LRU_WIDTH), CONV_W ** -0.5)
    conv_b = nrm(ks[9], (DEPTH, LRU_WIDTH), 0.02)
    lru_wa = nrm(ks[10], (DEPTH, LRU_BLOCKS, LRU_BW, LRU_BW), LRU_BW ** -0.5)
    lru_ba = nrm(ks[11], (DEPTH, LRU_WIDTH), 0.02)
    lru_wx = nrm(ks[12], (DEPTH, LRU_BLOCKS, LRU_BW, LRU_BW), LRU_BW ** -0.5)
    lru_bx = nrm(ks[13], (DEPTH, LRU_WIDTH), 0.02)
    u = jax.random.uniform(ks[14], (DEPTH, LRU_WIDTH), jnp.float32, 0.9, 0.999)
    a0 = u ** (1.0 / LRU_C)
    lru_lambda = jnp.log(a0) - jnp.log1p(-a0)
    qn_g = 1.0 + nrm(ks[15], (DEPTH, N_GROUPS, HEAD_DIM), 0.02)
    kn_g = 1.0 + nrm(ks[16], (DEPTH, N_GROUPS, HEAD_DIM), 0.02)
    gla_a1 = nrm(ks[17], (DEPTH, D, GLA_RANK), D ** -0.5)
    gla_a2 = nrm(ks[18], (DEPTH, GLA_RANK, GLA_KW), GLA_RANK ** -0.5)
    gla_ab = nrm(ks[19], (DEPTH, GLA_KW), 0.02)
    gla_on_g = 1.0 + nrm(ks[20], (DEPTH, GLA_DV), 0.02)
    proj_a = nrm(ks[21], (DEPTH, LRU_WIDTH, D), LRU_WIDTH ** -0.5)
    proj_b = nrm(ks[22], (DEPTH, ATT_OUT, D), ATT_OUT ** -0.5)
    kk = jax.random.split(ks[23], 2)
    proj_c = nrm(kk[0], (DEPTH, GLA_VW, D), GLA_VW ** -0.5)
    w_o = nrm(kk[1], (DEPTH, D, D), D ** -0.5)
    return {'x': x, 'c': c, 'positions': positions, 'ada_w': ada_w, 'ada_b': ada_b,
            'norm_g': norm_g, 'w_in': w_in, 'conv_w': conv_w, 'conv_b': conv_b,
            'lru_wa': lru_wa, 'lru_ba': lru_ba, 'lru_wx': lru_wx, 'lru_bx': lru_bx,
            'lru_lambda': lru_lambda, 'qn_g': qn_g, 'kn_g': kn_g, 'gla_a1': gla_a1,
            'gla_a2': gla_a2, 'gla_ab': gla_ab, 'gla_on_g': gla_on_g, 'proj_a': proj_a,
            'proj_b': proj_b, 'proj_c': proj_c, 'w_o': w_o}


def reference(x, c, positions, ada_w, ada_b, norm_g, w_in, conv_w, conv_b,
              lru_wa, lru_ba, lru_wx, lru_bx, lru_lambda, qn_g, kn_g,
              gla_a1, gla_a2, gla_ab, gla_on_g, proj_a, proj_b, proj_c, w_o):
    c_act = jax.nn.silu(c)
    for l in range(DEPTH):
        x = hybrid_layer(x, c_act, positions, ada_w[l], ada_b[l], norm_g[l], w_in[l],
                         conv_w[l], conv_b[l], lru_wa[l], lru_ba[l], lru_wx[l], lru_bx[l],
                         lru_lambda[l], qn_g[l], kn_g[l], gla_a1[l], gla_a2[l], gla_ab[l],
                         gla_on_g[l], proj_a[l], proj_b[l], proj_c[l], w_o[l])
    return x
```

```python
import functools

import jax
import jax.numpy as jnp
import numpy as np
from jax import lax
from jax.experimental import pallas as pl
from jax.experimental.pallas import tpu as pltpu

LRU_BLOCKS = 16
CONV_W = 4
LRU_C = 8.0
DIL_PATTERNS = ((128, 1), (512, 4), (2048, 16))
N_GROUPS = 3
HEADS_PER_GROUP = 8
HEAD_DIM = 64
ATT_OUT = HEADS_PER_GROUP * HEAD_DIM
BAND_BLK = 128
ROPE_THETA = 10000.0
GLA_HEADS = 4
GLA_RANK = 16
GLA_NORMALIZER = 16.0
GLA_CHUNK = 64
EPS = 1e-6

LANES = 128
SUBLANES = 8
MXU_DIM = 256
VMEM_LIMIT = 48 * 1024 * 1024

COL = 512
NEG_BIG = -1e30
BF16 = jnp.bfloat16
F32 = jnp.float32


def _params(sem):
    return pltpu.CompilerParams(dimension_semantics=sem, vmem_limit_bytes=VMEM_LIMIT)


def _split_bf16(x):
    hi = x.astype(BF16)
    lo = (x - hi.astype(F32)).astype(BF16)
    return hi, lo


def _dot(a, b):
    return jnp.dot(a, b, preferred_element_type=F32)


def _dot_nt(a, b):
    return lax.dot_general(a, b, (((1,), (1,)), ((), ())), preferred_element_type=F32)


def _dot_tn(a, b):
    return lax.dot_general(a, b, (((0,), (0,)), ((), ())), preferred_element_type=F32)


def _sigmoid(x):
    return 1.0 / (1.0 + jnp.exp(-x))


def _silu(x):
    return x * _sigmoid(x)


def _softplus(x):
    return jnp.maximum(x, 0.0) + jnp.log1p(jnp.exp(-jnp.abs(x)))


def _rope_kernel(pos_ref, inv_ref, sgn_ref, cos_ref, sin_ref):
    ang = pos_ref[...].astype(F32) * inv_ref[...]
    cos_ref[...] = jnp.cos(ang)
    sin_ref[...] = jnp.sin(ang) * sgn_ref[...]


def rope_tables(positions):
    T = positions.size
    half = HEAD_DIM // 2
    inv = ROPE_THETA ** (-(np.arange(half, dtype=np.float32) / half))
    lane = np.arange(LANES)
    inv_row = jnp.asarray(inv[lane % half].reshape(1, LANES), F32)
    sgn_row = jnp.asarray(np.where((lane % HEAD_DIM) < half, -1.0, 1.0).reshape(1, LANES), F32)
    rows = min(T, 2048)
    return pl.pallas_call(
        _rope_kernel,
        grid=(T // rows,),
        in_specs=[pl.BlockSpec((rows, 1), lambda i: (i, 0)),
                  pl.BlockSpec((1, LANES), lambda i: (0, 0)),
                  pl.BlockSpec((1, LANES), lambda i: (0, 0))],
        out_specs=[pl.BlockSpec((rows, LANES), lambda i: (i, 0)),
                   pl.BlockSpec((rows, LANES), lambda i: (i, 0))],
        out_shape=[jax.ShapeDtypeStruct((T, LANES), F32)] * 2,
        compiler_params=_params(("arbitrary",)),
        name="rope_tables",
    )(positions.reshape(T, 1), inv_row, sgn_row)


def _mod_kernel(c_ref, w_ref, b_ref, o_ref):
    ca = _silu(c_ref[...])
    c_hi, c_lo = _split_bf16(ca)
    w_hi, w_lo = _split_bf16(w_ref[...])
    acc = _dot(c_hi, w_hi) + _dot(c_lo, w_hi) + _dot(c_hi, w_lo)
    o_ref[...] = acc + b_ref[...]


def adaln_mod(c, ada_w, ada_b):
    depth, D, D3 = ada_w.shape
    B = c.shape[0]
    cp = jnp.pad(c, ((0, SUBLANES - B), (0, 0)))
    tn = 1024
    return pl.pallas_call(
        _mod_kernel,
        grid=(depth, D3 // tn),
        in_specs=[pl.BlockSpec((SUBLANES, D), lambda l, j: (0, 0)),
                  pl.BlockSpec((None, D, tn), lambda l, j: (l, 0, j)),
                  pl.BlockSpec((None, 1, tn), lambda l, j: (l, 0, j))],
        out_specs=pl.BlockSpec((None, SUBLANES, tn), lambda l, j: (l, 0, j)),
        out_shape=jax.ShapeDtypeStruct((depth, SUBLANES, D3), F32),
        compiler_params=_params(("arbitrary", "arbitrary")),
        name="adaln_mod",
    )(cp, ada_w, ada_b.reshape(depth, 1, D3))


def _inproj_kernel(x_ref, ng_ref, scale_ref, shift_ref, w_ref, cos_ref, sin_ref, gain_ref,
                   ones_ref, a1_ref, a2_ref, ab_ref, z_ref, la_ref, h_scr, *, qk_lo, q_hi, qk_hi):
    j = pl.program_id(1)

    @pl.when(j == 0)
    def _():
        xf = x_ref[...]
        ms = jnp.mean(xf * xf, axis=-1, keepdims=True)
        y = xf * lax.rsqrt(ms + EPS) * ng_ref[...]
        h = y * (1.0 + scale_ref[...]) + shift_ref[...]
        hb = h.astype(BF16)
        h_scr[...] = hb
        t = _dot(hb, a1_ref[...])
        t_hi, t_lo = _split_bf16(t)
        a2_hi, a2_lo = _split_bf16(a2_ref[...])
        pre = _dot(t_hi, a2_hi) + _dot(t_lo, a2_hi) + _dot(t_hi, a2_lo) + ab_ref[...]
        la_ref[...] = -_softplus(-pre) * (1.0 / GLA_NORMALIZER)

    acc = _dot(h_scr[...], w_ref[...])
    is_qk = jnp.logical_and(j >= qk_lo, j < qk_hi)

    @pl.when(jnp.logical_not(is_qk))
    def _():
        z_ref[...] = acc.astype(z_ref.dtype)

    @pl.when(is_qk)
    def _():
        sq = acc * acc
        sq_hi, sq_lo = _split_bf16(sq)
        ones = ones_ref[...]
        parts = []
        for c in range(COL // MXU_DIM):
            sl = slice(c * MXU_DIM, (c + 1) * MXU_DIM)
            parts.append(_dot(sq_hi[:, sl], ones) + _dot(sq_lo[:, sl], ones))
        ssq = jnp.concatenate(parts, axis=-1)
        xn = acc * lax.rsqrt(ssq * (1.0 / HEAD_DIM) + EPS) * gain_ref[...]
        qscale = jnp.where(j < q_hi, HEAD_DIM ** -0.5, 1.0).astype(F32)
        cos = cos_ref[...]
        sin = sin_ref[...]
        lane = lax.broadcasted_iota(jnp.int32, cos.shape, 1)
        first_half = (lane % HEAD_DIM) < (HEAD_DIM // 2)
        for c in range(COL // LANES):
            sl = slice(c * LANES, (c + 1) * LANES)
            xc = xn[:, sl]
            partner = jnp.where(first_half,
                                pltpu.roll(xc, LANES - HEAD_DIM // 2, 1),
                                pltpu.roll(xc, HEAD_DIM // 2, 1))
            z_ref[:, sl] = ((xc * cos + partner * sin) * qscale).astype(z_ref.dtype)


def in_projection(x2, mod, norm_g, w_in, cos_t, sin_t, qn_g, kn_g, gla_a1, gla_a2, gla_ab,
                  *, batch, tm):
    T, D = x2.shape
    n_in = w_in.shape[1]
    ncol = n_in // COL
    kw = gla_a2.shape[1]
    tiles_per_batch = T // batch // tm
    q_lo = 2 * (D // COL)
    n_att = N_GROUPS * ATT_OUT // COL
    q_hi = q_lo + n_att
    qk_hi = q_hi + n_att
    gains = jnp.ones((ncol, 1, COL), F32)
    gains = gains.at[q_lo:q_hi, 0].set(jnp.tile(qn_g, (1, HEADS_PER_GROUP)))
    gains = gains.at[q_hi:qk_hi, 0].set(jnp.tile(kn_g, (1, HEADS_PER_GROUP)))
    blk = np.arange(MXU_DIM) // HEAD_DIM
    ones_bd = jnp.asarray(blk[:, None] == blk[None, :], BF16)
    a1p = jnp.pad(gla_a1, ((0, 0), (0, LANES - GLA_RANK))).astype(BF16)
    a2p = jnp.pad(gla_a2, ((0, LANES - GLA_RANK), (0, 0)))
    scale = mod[:, None, D:2 * D]
    shift = mod[:, None, 0:D]
    kern = functools.partial(_inproj_kernel, qk_lo=q_lo, q_hi=q_hi, qk_hi=qk_hi)
    bmap = lambda i, j: (i // tiles_per_batch, 0, 0)
    return pl.pallas_call(
        kern,
        grid=(T // tm, ncol),
        in_specs=[pl.BlockSpec((tm, D), lambda i, j: (i, 0)),
                  pl.BlockSpec((1, D), lambda i, j: (0, 0)),
                  pl.BlockSpec((None, 1, D), bmap),
                  pl.BlockSpec((None, 1, D), bmap),
                  pl.BlockSpec((D, COL), lambda i, j: (0, j)),
                  pl.BlockSpec((tm, LANES), lambda i, j: (i, 0)),
                  pl.BlockSpec((tm, LANES), lambda i, j: (i, 0)),
                  pl.BlockSpec((None, 1, COL), lambda i, j: (j, 0, 0)),
                  pl.BlockSpec((MXU_DIM, MXU_DIM), lambda i, j: (0, 0)),
                  pl.BlockSpec((D, LANES), lambda i, j: (0, 0)),
                  pl.BlockSpec((LANES, kw), lambda i, j: (0, 0)),
                  pl.BlockSpec((1, kw), lambda i, j: (0, 0))],
        out_specs=[pl.BlockSpec((tm, COL), lambda i, j: (i, j)),
                   pl.BlockSpec((tm, kw), lambda i, j: (i, 0))],
        out_shape=[jax.ShapeDtypeStruct((T, n_in), BF16),
                   jax.ShapeDtypeStruct((T, kw), F32)],
        scratch_shapes=[pltpu.VMEM((tm, D), BF16)],
        compiler_params=_params(("arbitrary", "arbitrary")),
        name="in_projection",
    )(x2, norm_g.reshape(1, D), scale, shift, w_in.astype(BF16), cos_t, sin_t, gains, ones_bd,
      a1p, a2p, gla_ab.reshape(1, kw))


def _lru_kernel(ax_ref, ag_ref, cw_ref, cb_ref, wa_ref, ba_ref, wx_ref, bx_ref, lam_ref, y_ref,
                xe_scr, a_scr, u_scr, hc_scr, *, ts):
    s = pl.program_id(1)
    W = ax_ref.shape[-1]

    @pl.when(s == 0)
    def _():
        xe_scr[0:SUBLANES, :] = jnp.zeros((SUBLANES, W), F32)
        hc_scr[...] = jnp.zeros_like(hc_scr)

    @pl.when(s > 0)
    def _():
        xe_scr[0:SUBLANES, :] = xe_scr[ts:ts + SUBLANES, :]

    xcur = ax_ref[...].astype(F32)
    xe_scr[SUBLANES:SUBLANES + ts, :] = xcur
    xc = cb_ref[...] + xcur * cw_ref[CONV_W - 1:CONV_W, :]
    for k in range(CONV_W - 1):
        off = SUBLANES - (CONV_W - 1) + k
        xc = xc + xe_scr[off:off + ts, :] * cw_ref[k:k + 1, :]

    neg_c_sp = -LRU_C * _softplus(-lam_ref[...])
    for g in range(W // MXU_DIM):
        sl = slice(g * MXU_DIM, (g + 1) * MXU_DIM)
        xg = xc[:, sl]
        xb = xg.astype(BF16)
        r = _sigmoid(_dot(xb, wa_ref[g]) + ba_ref[:, sl])
        i = _sigmoid(_dot(xb, wx_ref[g]) + bx_ref[:, sl])
        log_a = neg_c_sp[:, sl] * r
        a = jnp.exp(log_a)
        a_scr[:, sl] = a
        u_scr[:, sl] = jnp.sqrt(jnp.tanh(-log_a) * (1.0 + a * a)) * (i * xg)

    row = lax.broadcasted_iota(jnp.int32, (SUBLANES, W), 0)

    def body(t, h_prev):
        rs = pl.ds(pl.multiple_of(t * SUBLANES, SUBLANES), SUBLANES)
        a = a_scr[rs, :]
        u = u_scr[rs, :]
        d = 1
        while d < SUBLANES:
            keep = row >= d
            a_sh = jnp.where(keep, pltpu.roll(a, d, 0), 1.0)
            u_sh = jnp.where(keep, pltpu.roll(u, d, 0), 0.0)
            u = a * u_sh + u
            a = a * a_sh
            d *= 2
        h = a * h_prev + u
        u_scr[rs, :] = h
        return jnp.broadcast_to(h[SUBLANES - 1:SUBLANES, :], (SUBLANES, W))

    hc_scr[...] = lax.fori_loop(0, ts // SUBLANES, body, hc_scr[...])
    y_ref[...] = (u_scr[...] * _silu(ag_ref[...].astype(F32))).astype(y_ref.dtype)


def lru_branch(z3, conv_w, conv_b, lru_wa, lru_ba, lru_wx, lru_bx, lru_lambda, *, ts):
    B, S, _ = z3.shape
    W = conv_w.shape[1]
    bw = W // LRU_BLOCKS
    per = MXU_DIM // bw

    def block_diag(w):
        w4 = w.reshape(W // MXU_DIM, per, bw, bw)
        eye = jnp.eye(per, dtype=w.dtype)
        return jnp.einsum('gpjk,pq->gpjqk', w4, eye).reshape(W // MXU_DIM, MXU_DIM, MXU_DIM).astype(BF16)

    row = lambda v: v.reshape(1, W)
    full = lambda shape: pl.BlockSpec(shape, lambda b, s: (0,) * len(shape))
    return pl.pallas_call(
        functools.partial(_lru_kernel, ts=ts),
        grid=(B, S // ts),
        in_specs=[pl.BlockSpec((None, ts, W), lambda b, s: (b, s, 0)),
                  pl.BlockSpec((None, ts, W), lambda b, s: (b, s, 1)),
                  full((CONV_W, W)), full((1, W)),
                  full((W // MXU_DIM, MXU_DIM, MXU_DIM)), full((1, W)),
                  full((W // MXU_DIM, MXU_DIM, MXU_DIM)), full((1, W)), full((1, W))],
        out_specs=pl.BlockSpec((None, ts, W), lambda b, s: (b, s, 0)),
        out_shape=jax.ShapeDtypeStruct((B, S, W), BF16),
        scratch_shapes=[pltpu.VMEM((ts + SUBLANES, W), F32), pltpu.VMEM((ts, W), F32),
                        pltpu.VMEM((ts, W), F32), pltpu.VMEM((SUBLANES, W), F32)],
        compiler_params=_params(("arbitrary", "arbitrary")),
        name="lru_branch",
    )(z3, z3, conv_w, row(conv_b), block_diag(lru_wa), row(lru_ba), block_diag(lru_wx),
      row(lru_bx), row(lru_lambda))


def _attn_kernel(q_ref, kp_ref, kc_ref, vp_ref, vc_ref, o_ref, lse_ref, *, span):
    n = pl.program_id(2)
    blk = BAND_BLK
    qi = lax.broadcasted_iota(jnp.int32, (blk, 2 * blk), 0)
    kj = lax.broadcasted_iota(jnp.int32, (blk, 2 * blk), 1)
    rel = qi + blk - kj
    ok = (rel >= 0) & (rel <= span) & ((kj >= blk) | (n > 0))
    bias = jnp.where(ok, 0.0, NEG_BIG).astype(F32)
    q = q_ref[...]
    k = jnp.concatenate([kp_ref[...], kc_ref[...]], axis=0)
    v = jnp.concatenate([vp_ref[...], vc_ref[...]], axis=0)
    lane = lax.broadcasted_iota(jnp.int32, (blk, LANES), 1)
    lanes_per_head = LANES // HEADS_PER_GROUP
    lse_tile = jnp.zeros((blk, LANES), F32)
    for h in range(HEADS_PER_GROUP):
        sl = slice(h * HEAD_DIM, (h + 1) * HEAD_DIM)
        s = _dot_nt(q[:, sl], k[:, sl]) + bias
        m = jnp.max(s, axis=-1, keepdims=True)
        p = jnp.exp(s - m)
        l = jnp.sum(p, axis=-1, keepdims=True)
        o = _dot(p.astype(BF16), v[:, sl]) / l
        o_ref[:, sl] = o.astype(o_ref.dtype)
        lse_tile = jnp.where(lane // lanes_per_head == h, m + jnp.log(l), lse_tile)
    lse_ref[...] = lse_tile


def attn_group(z3, g, window, dil, *, d_model):
    B, S, n_in = z3.shape
    assert S % (dil * BAND_BLK) == 0
    L = S // dil
    ncol = n_in // COL
    q_col = 2 * (d_model // COL) + g
    k_col = q_col + N_GROUPS
    v_col = k_col + N_GROUPS
    zr = z3.reshape(B, L, dil * n_in)
    blk = (None, BAND_BLK, COL)
    cur = lambda col: pl.BlockSpec(blk, lambda b, r, n: (b, n, r * ncol + col))
    prev = lambda col: pl.BlockSpec(blk, lambda b, r, n: (b, jnp.maximum(n - 1, 0), r * ncol + col))
    o, lse = pl.pallas_call(
        functools.partial(_attn_kernel, span=window // dil),
        grid=(B, dil, L // BAND_BLK),
        in_specs=[cur(q_col), prev(k_col), cur(k_col), prev(v_col), cur(v_col)],
        out_specs=[pl.BlockSpec(blk, lambda b, r, n: (b, n, r)),
                   pl.BlockSpec((None, BAND_BLK, LANES), lambda b, r, n: (b, n, r))],
        out_shape=[jax.ShapeDtypeStruct((B, L, dil * COL), BF16),
                   jax.ShapeDtypeStruct((B, L, dil * LANES), F32)],
        compiler_params=_params(("arbitrary", "arbitrary", "arbitrary")),
        name=f"attn_group{g}",
    )(zr, zr, zr, zr, zr)
    return o.reshape(B, S, COL), lse.reshape(B, S, LANES)


def _gla_kernel(q_ref, k_ref, v_ref, g_ref, la_ref, og_ref, y_ref, st_scr, *, ts):
    s = pl.program_id(1)
    C = GLA_CHUNK
    kw = q_ref.shape[-1]
    dk = kw // GLA_HEADS
    dv = v_ref.shape[-1] // GLA_HEADS

    @pl.when(s == 0)
    def _():
        st_scr[...] = jnp.zeros_like(st_scr)

    row = lax.broadcasted_iota(jnp.int32, (C, kw), 0)
    qi = lax.broadcasted_iota(jnp.int32, (C, C), 0)
    kj = lax.broadcasted_iota(jnp.int32, (C, C), 1)
    causal = qi >= kj
    for c in range(ts // C):
        rs = slice(c * C, (c + 1) * C)
        b = la_ref[rs, :]
        d = 1
        while d < C:
            b = b + jnp.where(row >= d, pltpu.roll(b, d, 0), 0.0)
            d *= 2
        b_last = b[C - 1:C, :]
        qf = q_ref[rs, :].astype(F32)
        kf = k_ref[rs, :].astype(F32)
        q_dec = (qf * (dk ** -0.5) * jnp.exp(b)).astype(BF16)
        k_inv = (kf * jnp.exp(-b)).astype(BF16)
        k_end = (kf * jnp.exp(b_last - b)).astype(BF16)
        dec = jnp.exp(b_last)
        for h in range(GLA_HEADS):
            ks = slice(h * dk, (h + 1) * dk)
            vs = slice(h * dv, (h + 1) * dv)
            vh = v_ref[rs, vs]
            att = jnp.where(causal, _dot_nt(q_dec[:, ks], k_inv[:, ks]), 0.0)
            st = st_scr[h]
            o = _dot(att.astype(BF16), vh) + _dot_nt(q_dec[:, ks], st.astype(BF16))
            st_scr[h] = st * dec[:, ks] + _dot_tn(vh, k_end[:, ks])
            ms = jnp.mean(o * o, axis=-1, keepdims=True)
            yn = o * lax.rsqrt(ms + EPS) * og_ref[...]
            y_ref[rs, vs] = (yn * _silu(g_ref[rs, vs].astype(F32))).astype(y_ref.dtype)


def gla_branch(z3, la3, gla_on_g, *, ts):
    B, S, _ = z3.shape
    kw = la3.shape[-1]
    dv = gla_on_g.shape[0]
    vw = GLA_HEADS * dv
    dk = kw // GLA_HEADS
    D = vw
    cq = (2 * D + 3 * N_GROUPS * ATT_OUT + ATT_OUT)
    assert cq % kw == 0 and (cq + 2 * kw) % vw == 0
    zmap = lambda width, col: pl.BlockSpec((None, ts, width), lambda b, s: (b, s, col))
    return pl.pallas_call(
        functools.partial(_gla_kernel, ts=ts),
        grid=(B, S // ts),
        in_specs=[zmap(kw, cq // kw), zmap(kw, cq // kw + 1),
                  zmap(vw, (cq + 2 * kw) // vw), zmap(vw, (cq + 2 * kw) // vw + 1),
                  pl.BlockSpec((None, ts, kw), lambda b, s: (b, s, 0)),
                  pl.BlockSpec((1, dv), lambda b, s: (0, 0))],
        out_specs=pl.BlockSpec((None, ts, vw), lambda b, s: (b, s, 0)),
        out_shape=jax.ShapeDtypeStruct((B, S, vw), BF16),
        scratch_shapes=[pltpu.VMEM((GLA_HEADS, dv, dk), F32)],
        compiler_params=_params(("arbitrary", "arbitrary")),
        name="gla_branch",
    )(z3, z3, z3, z3, la3, gla_on_g.reshape(1, dv))


def _merge_kernel(x_ref, gate_ref, ya_ref, o0_ref, o1_ref, o2_ref, l0_ref, l1_ref, l2_ref,
                  bg_ref, yc_ref, m0_ref, m1_ref, m2_ref, pa_ref, pb_ref, pc_ref, wo_ref, out_ref):
    l0, l1, l2 = l0_ref[...], l1_ref[...], l2_ref[...]
    mx = jnp.maximum(jnp.maximum(l0, l1), l2)
    e0, e1, e2 = jnp.exp(l0 - mx), jnp.exp(l1 - mx), jnp.exp(l2 - mx)
    den = e0 + e1 + e2
    wts = (e0 / den, e1 / den, e2 / den)
    tm = x_ref.shape[0]
    lanes_per_head = LANES // HEADS_PER_GROUP
    lane = lax.broadcasted_iota(jnp.int32, (tm, LANES), 1)
    upper = lane >= HEAD_DIM
    cols = []
    for c in range(ATT_OUT // LANES):
        sl = slice(c * LANES, (c + 1) * LANES)
        acc = jnp.zeros((tm, LANES), F32)
        for w, o_ref in zip(wts, (o0_ref, o1_ref, o2_ref)):
            h0 = 2 * c * lanes_per_head
            h1 = (2 * c + 1) * lanes_per_head
            wv = jnp.where(upper, w[:, h1:h1 + 1], w[:, h0:h0 + 1])
            acc = acc + wv * o_ref[:, sl].astype(F32)
        cols.append(acc)
    yb = jnp.concatenate(cols, axis=-1) * _silu(bg_ref[...].astype(F32))
    merged = (_sigmoid(m0_ref[...].astype(F32)) * _dot(ya_ref[...], pa_ref[...])
              + _sigmoid(m1_ref[...].astype(F32)) * _dot(yb.astype(BF16), pb_ref[...])
              + _sigmoid(m2_ref[...].astype(F32)) * _dot(yc_ref[...], pc_ref[...]))
    out_ref[...] = x_ref[...] + gate_ref[...] * _dot(merged.astype(BF16), wo_ref[...])


def merge_layer(x2, mod, ya, o_groups, lse_groups, yc, z2, proj_a, proj_b, proj_c, w_o,
                *, batch, tm):
    T, D = x2.shape
    n_in = z2.shape[1]
    tiles_per_batch = T // batch // tm
    gate = mod[:, None, 2 * D:3 * D]
    bg_col = (2 * D + 3 * N_GROUPS * ATT_OUT) // COL
    mg_col = (n_in - 3 * D) // D
    assert (n_in - 3 * D) % D == 0
    rows = lambda width, col=0: pl.BlockSpec((tm, width), lambda i: (i, col))
    full = lambda a: pl.BlockSpec(a.shape, lambda i: (0, 0))
    weights = [w.astype(BF16) for w in (proj_a, proj_b, proj_c, w_o)]
    return pl.pallas_call(
        _merge_kernel,
        grid=(T // tm,),
        in_specs=[rows(D), pl.BlockSpec((None, 1, D), lambda i: (i // tiles_per_batch, 0, 0)),
                  rows(D), rows(COL), rows(COL), rows(COL), rows(LANES), rows(LANES), rows(LANES),
                  rows(COL, bg_col), rows(D), rows(D, mg_col), rows(D, mg_col + 1),
                  rows(D, mg_col + 2)] + [full(w) for w in weights],
        out_specs=rows(D),
        out_shape=jax.ShapeDtypeStruct((T, D), F32),
        compiler_params=_params(("arbitrary",)),
        name="merge_layer",
    )(x2, gate, ya, *o_groups, *lse_groups, z2, yc, z2, z2, z2, *weights)


def kernel(x, c, positions, ada_w, ada_b, norm_g, w_in, conv_w, conv_b, lru_wa, lru_ba, lru_wx,
           lru_bx, lru_lambda, qn_g, kn_g, gla_a1, gla_a2, gla_ab, gla_on_g, proj_a, proj_b,
           proj_c, w_o):
    B, S, D = x.shape
    T = B * S
    depth = ada_w.shape[0]
    tm_proj = min(1024, S)
    tm_merge = min(512, S)
    ts_seq = min(512, S)
    cos_t, sin_t = rope_tables(positions)
    mods = adaln_mod(c, ada_w, ada_b)
    x2 = x.reshape(T, D)
    for l in range(depth):
        z2, la = in_projection(x2, mods[l], norm_g[l], w_in[l], cos_t, sin_t, qn_g[l], kn_g[l],
                               gla_a1[l], gla_a2[l], gla_ab[l], batch=B, tm=tm_proj)
        z3 = z2.reshape(B, S, -1)
        ya = lru_branch(z3, conv_w[l], conv_b[l], lru_wa[l], lru_ba[l], lru_wx[l], lru_bx[l],
                        lru_lambda[l], ts=ts_seq)
        outs = [attn_group(z3, g, win, dil, d_model=D) for g, (win, dil) in enumerate(DIL_PATTERNS)]
        yc = gla_branch(z3, la.reshape(B, S, -1), gla_on_g[l], ts=ts_seq)
        x2 = merge_layer(x2, mods[l], ya.reshape(T, -1),
                         [o.reshape(T, -1) for o, _ in outs], [s.reshape(T, -1) for _, s in outs],
                         yc.reshape(T, -1), z2, proj_a[l], proj_b[l], proj_c[l], w_o[l],
                         batch=B, tm=tm_merge)
    return x2.reshape(B, S, D)
```

```python
import functools

import jax
import jax.numpy as jnp
import numpy as np
from jax import lax
from jax.experimental import pallas as pl
from jax.experimental.pallas import tpu as pltpu

LRU_BLOCKS = 16
CONV_W = 4
LRU_C = 8.0
DIL_PATTERNS = ((128, 1), (512, 4), (2048, 16))
N_GROUPS = 3
HEADS_PER_GROUP = 8
HEAD_DIM = 64
ATT_OUT = HEADS_PER_GROUP * HEAD_DIM
BAND_BLK = 128
ROPE_THETA = 10000.0
GLA_HEADS = 4
GLA_RANK = 16
GLA_NORMALIZER = 16.0
GLA_CHUNK = 64
EPS = 1e-6

LANES = 128
SUBLANES = 8
MXU_DIM = 256
VMEM_LIMIT = 48 * 1024 * 1024

COL = 512
N_ATT_TILES = 3 * N_GROUPS
NEG_BIG = -1e30
BF16 = jnp.bfloat16
F32 = jnp.float32


def _params(sem):
    return pltpu.CompilerParams(dimension_semantics=sem, vmem_limit_bytes=VMEM_LIMIT)


def _split_bf16(x):
    hi = x.astype(BF16)
    lo = (x - hi.astype(F32)).astype(BF16)
    return hi, lo


def _dot(a, b):
    return jnp.dot(a, b, preferred_element_type=F32)


def _dot_nt(a, b):
    return lax.dot_general(a, b, (((1,), (1,)), ((), ())), preferred_element_type=F32)


def _dot_tn(a, b):
    return lax.dot_general(a, b, (((0,), (0,)), ((), ())), preferred_element_type=F32)


def _sigmoid(x):
    return 1.0 / (1.0 + jnp.exp(-x))


def _silu(x):
    return x * _sigmoid(x)


def _softplus(x):
    return jnp.maximum(x, 0.0) + jnp.log1p(jnp.exp(-jnp.abs(x)))


def _rope_kernel(pos_ref, inv_ref, sgn_ref, cos_ref, sin_ref):
    ang = pos_ref[...].astype(F32) * inv_ref[...]
    cos_ref[...] = jnp.cos(ang)
    sin_ref[...] = jnp.sin(ang) * sgn_ref[...]


def rope_tables(positions):
    T = positions.size
    half = HEAD_DIM // 2
    inv = ROPE_THETA ** (-(np.arange(half, dtype=np.float32) / half))
    lane = np.arange(LANES)
    inv_row = jnp.asarray(inv[lane % half].reshape(1, LANES), F32)
    sgn_row = jnp.asarray(np.where((lane % HEAD_DIM) < half, -1.0, 1.0).reshape(1, LANES), F32)
    rows = min(T, 2048)
    return pl.pallas_call(
        _rope_kernel,
        grid=(T // rows,),
        in_specs=[pl.BlockSpec((rows, 1), lambda i: (i, 0)),
                  pl.BlockSpec((1, LANES), lambda i: (0, 0)),
                  pl.BlockSpec((1, LANES), lambda i: (0, 0))],
        out_specs=[pl.BlockSpec((rows, LANES), lambda i: (i, 0)),
                   pl.BlockSpec((rows, LANES), lambda i: (i, 0))],
        out_shape=[jax.ShapeDtypeStruct((T, LANES), F32)] * 2,
        compiler_params=_params(("arbitrary",)),
        name="rope_tables",
    )(positions.reshape(T, 1), inv_row, sgn_row)


def _mod_kernel(c_ref, w_ref, b_ref, o_ref):
    ca = _silu(c_ref[...])
    c_hi, c_lo = _split_bf16(ca)
    w_hi, w_lo = _split_bf16(w_ref[...])
    acc = _dot(c_hi, w_hi) + _dot(c_lo, w_hi) + _dot(c_hi, w_lo)
    o_ref[...] = acc + b_ref[...]


def adaln_mod(c, ada_w, ada_b):
    depth, D, D3 = ada_w.shape
    B = c.shape[0]
    cp = jnp.pad(c, ((0, SUBLANES - B), (0, 0)))
    tn = 1024
    return pl.pallas_call(
        _mod_kernel,
        grid=(depth, D3 // tn),
        in_specs=[pl.BlockSpec((SUBLANES, D), lambda l, j: (0, 0)),
                  pl.BlockSpec((None, D, tn), lambda l, j: (l, 0, j)),
                  pl.BlockSpec((None, 1, tn), lambda l, j: (l, 0, j))],
        out_specs=pl.BlockSpec((None, SUBLANES, tn), lambda l, j: (l, 0, j)),
        out_shape=jax.ShapeDtypeStruct((depth, SUBLANES, D3), F32),
        compiler_params=_params(("arbitrary", "arbitrary")),
        name="adaln_mod",
    )(cp, ada_w, ada_b.reshape(depth, 1, D3))


def _inproj_kernel(x_ref, ng_ref, scale_ref, shift_ref, w_ref, cos_ref, sin_ref, gain_ref,
                   ones_ref, a1_ref, a2_ref, ab_ref, qkv0_ref, qkv1_ref, qkv2_ref, z_ref, la_ref,
                   h_scr, val_scr):
    j = pl.program_id(1)
    tm = x_ref.shape[0]

    @pl.when(j == 0)
    def _():
        xf = x_ref[...]
        ms = jnp.mean(xf * xf, axis=-1, keepdims=True)
        y = xf * lax.rsqrt(ms + EPS) * ng_ref[...]
        h = y * (1.0 + scale_ref[...]) + shift_ref[...]
        hb = h.astype(BF16)
        h_scr[...] = hb
        t = _dot(hb, a1_ref[...])
        t_hi, t_lo = _split_bf16(t)
        a2_hi, a2_lo = _split_bf16(a2_ref[...])
        pre = _dot(t_hi, a2_hi) + _dot(t_lo, a2_hi) + _dot(t_hi, a2_lo) + ab_ref[...]
        la_ref[...] = -_softplus(-pre) * (1.0 / GLA_NORMALIZER)

    acc = _dot(h_scr[...], w_ref[...])
    kind = j % 3
    is_att = j < N_ATT_TILES

    @pl.when(jnp.logical_not(is_att))
    def _():
        z_ref[...] = acc.astype(z_ref.dtype)

    @pl.when(jnp.logical_and(is_att, kind == 2))
    def _():
        for c in range(COL // LANES):
            val_scr[c] = acc[:, c * LANES:(c + 1) * LANES]

    @pl.when(jnp.logical_and(is_att, kind < 2))
    def _():
        sq = acc * acc
        sq_hi, sq_lo = _split_bf16(sq)
        ones = ones_ref[...]
        parts = []
        for c in range(COL // MXU_DIM):
            sl = slice(c * MXU_DIM, (c + 1) * MXU_DIM)
            parts.append(_dot(sq_hi[:, sl], ones) + _dot(sq_lo[:, sl], ones))
        ssq = jnp.concatenate(parts, axis=-1)
        xn = acc * lax.rsqrt(ssq * (1.0 / HEAD_DIM) + EPS) * gain_ref[...]
        qscale = jnp.where(kind == 0, HEAD_DIM ** -0.5, 1.0).astype(F32)
        cos = cos_ref[...]
        sin = sin_ref[...]
        lane = lax.broadcasted_iota(jnp.int32, cos.shape, 1)
        first_half = (lane % HEAD_DIM) < (HEAD_DIM // 2)
        for c in range(COL // LANES):
            sl = slice(c * LANES, (c + 1) * LANES)
            xc = xn[:, sl]
            partner = jnp.where(first_half,
                                pltpu.roll(xc, LANES - HEAD_DIM // 2, 1),
                                pltpu.roll(xc, HEAD_DIM // 2, 1))
            val_scr[c] = (xc * cos + partner * sin) * qscale

    for g, out_ref in enumerate((qkv0_ref, qkv1_ref, qkv2_ref)):
        dil = DIL_PATTERNS[g][1]

        @pl.when(j // 3 == g)
        def _(dil=dil, out_ref=out_ref):
            for c in range(COL // LANES):
                sl = slice(c * LANES, (c + 1) * LANES)
                for r in range(dil):
                    rows = pl.ds(r, tm // dil, stride=dil) if dil > 1 else slice(None)
                    out_ref[r, :, sl] = val_scr[c, rows, :].astype(out_ref.dtype)


def _permuted_w_in(w_in, D):
    att = N_GROUPS * ATT_OUT
    b_q, b_k, b_v, b_g = 2 * D, 2 * D + att, 2 * D + 2 * att, 2 * D + 3 * att
    c_q = b_g + ATT_OUT
    c_k = c_q + COL
    c_v = c_k + COL
    c_g = c_v + D
    m_g = c_g + D
    segs = []
    for g in range(N_GROUPS):
        segs += [(b_q + g * ATT_OUT, ATT_OUT), (b_k + g * ATT_OUT, ATT_OUT), (b_v + g * ATT_OUT, ATT_OUT)]
    segs += [(0, D), (D, D), (c_v, D), (c_g, D), (m_g, 3 * D), (b_g, ATT_OUT), (c_q, COL), (c_k, COL)]
    assert sum(w for _, w in segs) == w_in.shape[1]
    return jnp.concatenate([w_in[:, s:s + w] for s, w in segs], axis=1).astype(BF16)


def in_projection(x2, mod, norm_g, w_in, cos_t, sin_t, qn_g, kn_g, gla_a1, gla_a2, gla_ab,
                  *, batch, tm):
    T, D = x2.shape
    S = T // batch
    n_in = w_in.shape[1]
    ncol = n_in // COL
    kw = gla_a2.shape[1]
    tiles_per_batch = S // tm
    gains = jnp.ones((N_GROUPS, 3, COL), F32)
    gains = gains.at[:, 0].set(jnp.tile(qn_g, (1, HEADS_PER_GROUP)))
    gains = gains.at[:, 1].set(jnp.tile(kn_g, (1, HEADS_PER_GROUP)))
    gains = gains.reshape(N_ATT_TILES, 1, COL)
    blk = np.arange(MXU_DIM) // HEAD_DIM
    ones_bd = jnp.asarray(blk[:, None] == blk[None, :], BF16)
    a1p = jnp.pad(gla_a1, ((0, 0), (0, LANES - GLA_RANK))).astype(BF16)
    a2p = jnp.pad(gla_a2, ((0, LANES - GLA_RANK), (0, 0)))
    scale = mod[:, None, D:2 * D]
    shift = mod[:, None, 0:D]
    bmap = lambda i, j: (i // tiles_per_batch, 0, 0)

    def qkv_spec(g):
        dil = DIL_PATTERNS[g][1]
        assert tm % (dil * 16) == 0
        return pl.BlockSpec(
            (None, None, dil, tm // dil, COL),
            lambda i, j: (i // tiles_per_batch, jnp.clip(j - 3 * g, 0, 2), 0, i % tiles_per_batch, 0))

    def qkv_shape(g):
        dil = DIL_PATTERNS[g][1]
        return jax.ShapeDtypeStruct((batch, 3, dil, S // dil, COL), BF16)

    return pl.pallas_call(
        _inproj_kernel,
        grid=(T // tm, ncol),
        in_specs=[pl.BlockSpec((tm, D), lambda i, j: (i, 0)),
                  pl.BlockSpec((1, D), lambda i, j: (0, 0)),
                  pl.BlockSpec((None, 1, D), bmap),
                  pl.BlockSpec((None, 1, D), bmap),
                  pl.BlockSpec((D, COL), lambda i, j: (0, j)),
                  pl.BlockSpec((tm, LANES), lambda i, j: (i, 0)),
                  pl.BlockSpec((tm, LANES), lambda i, j: (i, 0)),
                  pl.BlockSpec((None, 1, COL), lambda i, j: (jnp.minimum(j, N_ATT_TILES - 1), 0, 0)),
                  pl.BlockSpec((MXU_DIM, MXU_DIM), lambda i, j: (0, 0)),
                  pl.BlockSpec((D, LANES), lambda i, j: (0, 0)),
                  pl.BlockSpec((LANES, kw), lambda i, j: (0, 0)),
                  pl.BlockSpec((1, kw), lambda i, j: (0, 0))],
        out_specs=[qkv_spec(0), qkv_spec(1), qkv_spec(2),
                   pl.BlockSpec((tm, COL), lambda i, j: (i, jnp.maximum(j - N_ATT_TILES, 0))),
                   pl.BlockSpec((tm, kw), lambda i, j: (i, 0))],
        out_shape=[qkv_shape(0), qkv_shape(1), qkv_shape(2),
                   jax.ShapeDtypeStruct((T, n_in - N_ATT_TILES * COL), BF16),
                   jax.ShapeDtypeStruct((T, kw), F32)],
        scratch_shapes=[pltpu.VMEM((tm, D), BF16), pltpu.VMEM((COL // LANES, tm, LANES), F32)],
        compiler_params=_params(("arbitrary", "arbitrary")),
        name="in_projection",
    )(x2, norm_g.reshape(1, D), scale, shift, _permuted_w_in(w_in, D), cos_t, sin_t, gains, ones_bd,
      a1p, a2p, gla_ab.reshape(1, kw))


def _lru_kernel(ax_ref, ag_ref, cw_ref, cb_ref, wa_ref, ba_ref, wx_ref, bx_ref, lam_ref, y_ref,
                xe_scr, a_scr, u_scr, hc_scr, *, ts):
    s = pl.program_id(1)
    W = ax_ref.shape[-1]

    @pl.when(s == 0)
    def _():
        xe_scr[0:SUBLANES, :] = jnp.zeros((SUBLANES, W), F32)
        hc_scr[...] = jnp.zeros_like(hc_scr)

    @pl.when(s > 0)
    def _():
        xe_scr[0:SUBLANES, :] = xe_scr[ts:ts + SUBLANES, :]

    xcur = ax_ref[...].astype(F32)
    xe_scr[SUBLANES:SUBLANES + ts, :] = xcur
    xc = cb_ref[...] + xcur * cw_ref[CONV_W - 1:CONV_W, :]
    for k in range(CONV_W - 1):
        off = SUBLANES - (CONV_W - 1) + k
        xc = xc + xe_scr[off:off + ts, :] * cw_ref[k:k + 1, :]

    neg_c_sp = -LRU_C * _softplus(-lam_ref[...])
    for g in range(W // MXU_DIM):
        sl = slice(g * MXU_DIM, (g + 1) * MXU_DIM)
        xg = xc[:, sl]
        xb = xg.astype(BF16)
        r = _sigmoid(_dot(xb, wa_ref[g]) + ba_ref[:, sl])
        i = _sigmoid(_dot(xb, wx_ref[g]) + bx_ref[:, sl])
        log_a = neg_c_sp[:, sl] * r
        a = jnp.exp(log_a)
        a_scr[:, sl] = a
        u_scr[:, sl] = jnp.sqrt(jnp.tanh(-log_a) * (1.0 + a * a)) * (i * xg)

    row = lax.broadcasted_iota(jnp.int32, (SUBLANES, W), 0)

    def body(t, h_prev):
        rs = pl.ds(pl.multiple_of(t * SUBLANES, SUBLANES), SUBLANES)
        a = a_scr[rs, :]
        u = u_scr[rs, :]
        d = 1
        while d < SUBLANES:
            keep = row >= d
            a_sh = jnp.where(keep, pltpu.roll(a, d, 0), 1.0)
            u_sh = jnp.where(keep, pltpu.roll(u, d, 0), 0.0)
            u = a * u_sh + u
            a = a * a_sh
            d *= 2
        h = a * h_prev + u
        u_scr[rs, :] = h
        return jnp.broadcast_to(h[SUBLANES - 1:SUBLANES, :], (SUBLANES, W))

    hc_scr[...] = lax.fori_loop(0, ts // SUBLANES, body, hc_scr[...])
    y_ref[...] = (u_scr[...] * _silu(ag_ref[...].astype(F32))).astype(y_ref.dtype)


def lru_branch(z3, conv_w, conv_b, lru_wa, lru_ba, lru_wx, lru_bx, lru_lambda, *, ts):
    B, S, _ = z3.shape
    W = conv_w.shape[1]
    bw = W // LRU_BLOCKS
    per = MXU_DIM // bw

    def block_diag(w):
        w4 = w.reshape(W // MXU_DIM, per, bw, bw)
        eye = jnp.eye(per, dtype=w.dtype)
        return jnp.einsum('gpjk,pq->gpjqk', w4, eye).reshape(W // MXU_DIM, MXU_DIM, MXU_DIM).astype(BF16)

    row = lambda v: v.reshape(1, W)
    full = lambda shape: pl.BlockSpec(shape, lambda b, s: (0,) * len(shape))
    return pl.pallas_call(
        functools.partial(_lru_kernel, ts=ts),
        grid=(B, S // ts),
        in_specs=[pl.BlockSpec((None, ts, W), lambda b, s: (b, s, 0)),
                  pl.BlockSpec((None, ts, W), lambda b, s: (b, s, 1)),
                  full((CONV_W, W)), full((1, W)),
                  full((W // MXU_DIM, MXU_DIM, MXU_DIM)), full((1, W)),
                  full((W // MXU_DIM, MXU_DIM, MXU_DIM)), full((1, W)), full((1, W))],
        out_specs=pl.BlockSpec((None, ts, W), lambda b, s: (b, s, 0)),
        out_shape=jax.ShapeDtypeStruct((B, S, W), BF16),
        scratch_shapes=[pltpu.VMEM((ts + SUBLANES, W), F32), pltpu.VMEM((ts, W), F32),
                        pltpu.VMEM((ts, W), F32), pltpu.VMEM((SUBLANES, W), F32)],
        compiler_params=_params(("arbitrary", "arbitrary")),
        name="lru_branch",
    )(z3, z3, conv_w, row(conv_b), block_diag(lru_wa), row(lru_ba), block_diag(lru_wx),
      row(lru_bx), row(lru_lambda))


def _attn_kernel(q_ref, kp_ref, kc_ref, vp_ref, vc_ref, o_ref, lse_ref, *, span):
    n = pl.program_id(2)
    blk = BAND_BLK
    qi = lax.broadcasted_iota(jnp.int32, (blk, 2 * blk), 0)
    kj = lax.broadcasted_iota(jnp.int32, (blk, 2 * blk), 1)
    rel = qi + blk - kj
    ok = (rel >= 0) & (rel <= span) & ((kj >= blk) | (n > 0))
    bias = jnp.where(ok, 0.0, NEG_BIG).astype(F32)
    q = q_ref[...]
    k = jnp.concatenate([kp_ref[...], kc_ref[...]], axis=0)
    v = jnp.concatenate([vp_ref[...], vc_ref[...]], axis=0)
    lane = lax.broadcasted_iota(jnp.int32, (blk, LANES), 1)
    lanes_per_head = LANES // HEADS_PER_GROUP
    lse_tile = jnp.zeros((blk, LANES), F32)
    for h in range(HEADS_PER_GROUP):
        sl = slice(h * HEAD_DIM, (h + 1) * HEAD_DIM)
        s = _dot_nt(q[:, sl], k[:, sl]) + bias
        m = jnp.max(s, axis=-1, keepdims=True)
        p = jnp.exp(s - m)
        l = jnp.sum(p, axis=-1, keepdims=True)
        o = _dot(p.astype(BF16), v[:, sl]) / l
        o_ref[:, sl] = o.astype(o_ref.dtype)
        lse_tile = jnp.where(lane // lanes_per_head == h, m + jnp.log(l), lse_tile)
    lse_ref[...] = lse_tile


def attn_group(qkv, g, window):
    B, _, dil, L, _ = qkv.shape
    assert L % BAND_BLK == 0
    blk = (None, None, None, BAND_BLK, COL)
    cur = lambda kind: pl.BlockSpec(blk, lambda b, r, n: (b, kind, r, n, 0))
    prev = lambda kind: pl.BlockSpec(blk, lambda b, r, n: (b, kind, r, jnp.maximum(n - 1, 0), 0))
    return pl.pallas_call(
        functools.partial(_attn_kernel, span=window // dil),
        grid=(B, dil, L // BAND_BLK),
        in_specs=[cur(0), prev(1), cur(1), prev(2), cur(2)],
        out_specs=[pl.BlockSpec((None, None, BAND_BLK, COL), lambda b, r, n: (b, r, n, 0)),
                   pl.BlockSpec((None, None, BAND_BLK, LANES), lambda b, r, n: (b, r, n, 0))],
        out_shape=[jax.ShapeDtypeStruct((B, dil, L, COL), BF16),
                   jax.ShapeDtypeStruct((B, dil, L, LANES), F32)],
        compiler_params=_params(("arbitrary", "arbitrary", "arbitrary")),
        name=f"attn_group{g}",
    )(qkv, qkv, qkv, qkv, qkv)


def _gla_kernel(q_ref, k_ref, v_ref, g_ref, la_ref, og_ref, y_ref, st_scr, *, ts):
    s = pl.program_id(1)
    C = GLA_CHUNK
    kw = q_ref.shape[-1]
    dk = kw // GLA_HEADS
    dv = v_ref.shape[-1] // GLA_HEADS

    @pl.when(s == 0)
    def _():
        st_scr[...] = jnp.zeros_like(st_scr)

    row = lax.broadcasted_iota(jnp.int32, (C, kw), 0)
    qi = lax.broadcasted_iota(jnp.int32, (C, C), 0)
    kj = lax.broadcasted_iota(jnp.int32, (C, C), 1)
    causal = qi >= kj
    for c in range(ts // C):
        rs = slice(c * C, (c + 1) * C)
        b = la_ref[rs, :]
        d = 1
        while d < C:
            b = b + jnp.where(row >= d, pltpu.roll(b, d, 0), 0.0)
            d *= 2
        b_last = b[C - 1:C, :]
        qf = q_ref[rs, :].astype(F32)
        kf = k_ref[rs, :].astype(F32)
        q_dec = (qf * (dk ** -0.5) * jnp.exp(b)).astype(BF16)
        k_inv = (kf * jnp.exp(-b)).astype(BF16)
        k_end = (kf * jnp.exp(b_last - b)).astype(BF16)
        dec = jnp.exp(b_last)
        for h in range(GLA_HEADS):
            ks = slice(h * dk, (h + 1) * dk)
            vs = slice(h * dv, (h + 1) * dv)
            vh = v_ref[rs, vs]
            att = jnp.where(causal, _dot_nt(q_dec[:, ks], k_inv[:, ks]), 0.0)
            st = st_scr[h]
            o = _dot(att.astype(BF16), vh) + _dot_nt(q_dec[:, ks], st.astype(BF16))
            st_scr[h] = st * dec[:, ks] + _dot_tn(vh, k_end[:, ks])
            ms = jnp.mean(o * o, axis=-1, keepdims=True)
            yn = o * lax.rsqrt(ms + EPS) * og_ref[...]
            y_ref[rs, vs] = (yn * _silu(g_ref[rs, vs].astype(F32))).astype(y_ref.dtype)


def gla_branch(z3, la3, gla_on_g, *, ts, d_model):
    B, S, _ = z3.shape
    kw = la3.shape[-1]
    dv = gla_on_g.shape[0]
    vw = GLA_HEADS * dv
    dk = kw // GLA_HEADS
    assert kw == COL and vw == d_model
    cq_col = (7 * d_model + ATT_OUT) // COL
    zmap = lambda width, col: pl.BlockSpec((None, ts, width), lambda b, s: (b, s, col))
    return pl.pallas_call(
        functools.partial(_gla_kernel, ts=ts),
        grid=(B, S // ts),
        in_specs=[zmap(kw, cq_col), zmap(kw, cq_col + 1), zmap(vw, 2), zmap(vw, 3),
                  pl.BlockSpec((None, ts, kw), lambda b, s: (b, s, 0)),
                  pl.BlockSpec((1, dv), lambda b, s: (0, 0))],
        out_specs=pl.BlockSpec((None, ts, vw), lambda b, s: (b, s, 0)),
        out_shape=jax.ShapeDtypeStruct((B, S, vw), BF16),
        scratch_shapes=[pltpu.VMEM((GLA_HEADS, dv, dk), F32)],
        compiler_params=_params(("arbitrary", "arbitrary")),
        name="gla_branch",
    )(z3, z3, z3, z3, la3, gla_on_g.reshape(1, dv))


def _merge_kernel(x_ref, gate_ref, ya_ref, o0_ref, o1_ref, o2_ref, l0_ref, l1_ref, l2_ref,
                  bg_ref, yc_ref, m0_ref, m1_ref, m2_ref, pa_ref, pb_ref, pc_ref, wo_ref, out_ref,
                  o_scr, l_scr):
    tm = x_ref.shape[0]
    for g, (o_ref, l_ref) in enumerate(((o0_ref, l0_ref), (o1_ref, l1_ref), (o2_ref, l2_ref))):
        dil = DIL_PATTERNS[g][1]
        for r in range(dil):
            rows = pl.ds(r, tm // dil, stride=dil) if dil > 1 else slice(None)
            l_scr[g, rows, :] = l_ref[r]
            for c in range(ATT_OUT // LANES):
                o_scr[g, c, rows, :] = o_ref[r, :, c * LANES:(c + 1) * LANES].astype(F32)
    l0, l1, l2 = l_scr[0], l_scr[1], l_scr[2]
    mx = jnp.maximum(jnp.maximum(l0, l1), l2)
    e0, e1, e2 = jnp.exp(l0 - mx), jnp.exp(l1 - mx), jnp.exp(l2 - mx)
    den = e0 + e1 + e2
    wts = (e0 / den, e1 / den, e2 / den)
    lanes_per_head = LANES // HEADS_PER_GROUP
    lane = lax.broadcasted_iota(jnp.int32, (tm, LANES), 1)
    upper = lane >= HEAD_DIM
    cols = []
    for c in range(ATT_OUT // LANES):
        sl = slice(c * LANES, (c + 1) * LANES)
        acc = jnp.zeros((tm, LANES), F32)
        for g, w in enumerate(wts):
            h0 = 2 * c * lanes_per_head
            h1 = (2 * c + 1) * lanes_per_head
            wv = jnp.where(upper, w[:, h1:h1 + 1], w[:, h0:h0 + 1])
            acc = acc + wv * o_scr[g, c]
        cols.append(acc)
    yb = jnp.concatenate(cols, axis=-1) * _silu(bg_ref[...].astype(F32))
    merged = (_sigmoid(m0_ref[...].astype(F32)) * _dot(ya_ref[...], pa_ref[...])
              + _sigmoid(m1_ref[...].astype(F32)) * _dot(yb.astype(BF16), pb_ref[...])
              + _sigmoid(m2_ref[...].astype(F32)) * _dot(yc_ref[...], pc_ref[...]))
    out_ref[...] = x_ref[...] + gate_ref[...] * _dot(merged.astype(BF16), wo_ref[...])


def merge_layer(x2, mod, ya, o_groups, lse_groups, yc, z2, proj_a, proj_b, proj_c, w_o,
                *, batch, tm):
    T, D = x2.shape
    tiles_per_batch = T // batch // tm
    gate = mod[:, None, 2 * D:3 * D]
    mg_col = 4
    bg_col = 7 * D // COL
    rows = lambda width, col=0: pl.BlockSpec((tm, width), lambda i: (i, col))
    full = lambda a: pl.BlockSpec(a.shape, lambda i: (0, 0))

    def residue_major(g, width):
        dil = DIL_PATTERNS[g][1]
        assert tm % (dil * 16) == 0
        return pl.BlockSpec((None, dil, tm // dil, width),
                            lambda i: (i // tiles_per_batch, 0, i % tiles_per_batch, 0))

    weights = [w.astype(BF16) for w in (proj_a, proj_b, proj_c, w_o)]
    return pl.pallas_call(
        _merge_kernel,
        grid=(T // tm,),
        in_specs=[rows(D), pl.BlockSpec((None, 1, D), lambda i: (i // tiles_per_batch, 0, 0)),
                  rows(D)] + [residue_major(g, COL) for g in range(N_GROUPS)]
                 + [residue_major(g, LANES) for g in range(N_GROUPS)]
                 + [rows(COL, bg_col), rows(D), rows(D, mg_col), rows(D, mg_col + 1),
                    rows(D, mg_col + 2)] + [full(w) for w in weights],
        out_specs=rows(D),
        out_shape=jax.ShapeDtypeStruct((T, D), F32),
        scratch_shapes=[pltpu.VMEM((N_GROUPS, ATT_OUT // LANES, tm, LANES), F32),
                        pltpu.VMEM((N_GROUPS, tm, LANES), F32)],
        compiler_params=_params(("arbitrary",)),
        name="merge_layer",
    )(x2, gate, ya, *o_groups, *lse_groups, z2, yc, z2, z2, z2, *weights)


def kernel(x, c, positions, ada_w, ada_b, norm_g, w_in, conv_w, conv_b, lru_wa, lru_ba, lru_wx,
           lru_bx, lru_lambda, qn_g, kn_g, gla_a1, gla_a2, gla_ab, gla_on_g, proj_a, proj_b,
           proj_c, w_o):
    B, S, D = x.shape
    T = B * S
    depth = ada_w.shape[0]
    tm_proj = min(1024, S)
    tm_merge = min(512, S)
    ts_seq = min(512, S)
    cos_t, sin_t = rope_tables(positions)
    mods = adaln_mod(c, ada_w, ada_b)
    x2 = x.reshape(T, D)
    for l in range(depth):
        *qkv, z2, la = in_projection(x2, mods[l], norm_g[l], w_in[l], cos_t, sin_t, qn_g[l],
                                     kn_g[l], gla_a1[l], gla_a2[l], gla_ab[l], batch=B, tm=tm_proj)
        z3 = z2.reshape(B, S, -1)
        ya = lru_branch(z3, conv_w[l], conv_b[l], lru_wa[l], lru_ba[l], lru_wx[l], lru_bx[l],
                        lru_lambda[l], ts=ts_seq)
        outs = [attn_group(qkv[g], g, win) for g, (win, _) in enumerate(DIL_PATTERNS)]
        yc = gla_branch(z3, la.reshape(B, S, -1), gla_on_g[l], ts=ts_seq, d_model=D)
        x2 = merge_layer(x2, mods[l], ya.reshape(T, -1), [o for o, _ in outs],
                         [s for _, s in outs], yc.reshape(T, -1), z2, proj_a[l], proj_b[l],
                         proj_c[l], w_o[l], batch=B, tm=tm_merge)
    return x2.reshape(B, S, D)
```

```python
import functools

import jax
import jax.numpy as jnp
import numpy as np
from jax import lax
from jax.experimental import pallas as pl
from jax.experimental.pallas import tpu as pltpu

LRU_BLOCKS = 16
CONV_W = 4
LRU_C = 8.0
DIL_PATTERNS = ((128, 1), (512, 4), (2048, 16))
N_GROUPS = 3
HEADS_PER_GROUP = 8
HEAD_DIM = 64
HALF = HEAD_DIM // 2
ATT_OUT = HEADS_PER_GROUP * HEAD_DIM
BAND_BLK = 128
ROPE_THETA = 10000.0
GLA_HEADS = 4
GLA_RANK = 16
GLA_NORMALIZER = 16.0
GLA_CHUNK = 64
EPS = 1e-6

LANES = 128
SUBLANES = 8
BF16_ROWS = 16
MXU_DIM = 256
VMEM_LIMIT = 48 * 1024 * 1024

COL = 512
N_ATT_TILES = 3 * N_GROUPS
NEG_BIG = -1e30
BF16 = jnp.bfloat16
F32 = jnp.float32

_PAIR_PERM = np.concatenate([np.arange(0, HALF), np.arange(HEAD_DIM, HEAD_DIM + HALF),
                             np.arange(HALF, HEAD_DIM), np.arange(HEAD_DIM + HALF, 2 * HEAD_DIM)])


def _params(sem):
    return pltpu.CompilerParams(dimension_semantics=sem, vmem_limit_bytes=VMEM_LIMIT)


def _split_bf16(x):
    hi = x.astype(BF16)
    lo = (x - hi.astype(F32)).astype(BF16)
    return hi, lo


def _dot(a, b):
    return jnp.dot(a, b, preferred_element_type=F32)


def _dot_nt(a, b):
    return lax.dot_general(a, b, (((1,), (1,)), ((), ())), preferred_element_type=F32)


def _dot_tn(a, b):
    return lax.dot_general(a, b, (((0,), (0,)), ((), ())), preferred_element_type=F32)


def _sigmoid(x):
    return 0.5 * jnp.tanh(0.5 * x) + 0.5


def _silu(x):
    h = 0.5 * x
    return h + h * jnp.tanh(h)


def _softplus(x):
    return jnp.maximum(x, 0.0) + jnp.log1p(jnp.exp(-jnp.abs(x)))


def _pair_permute_cols(w):
    shp = w.shape
    return w.reshape(shp[:-1] + (shp[-1] // LANES, LANES))[..., _PAIR_PERM].reshape(shp)


def _rope_kernel(pos_ref, inv_ref, sgn_ref, cos_ref, sin_ref):
    ang = pos_ref[...].astype(F32) * inv_ref[...]
    cos_ref[...] = jnp.cos(ang)
    sin_ref[...] = jnp.sin(ang) * sgn_ref[...]


def rope_tables(positions):
    T = positions.size
    inv = ROPE_THETA ** (-(np.arange(HALF, dtype=np.float32) / HALF))
    lane = np.arange(LANES)
    inv_row = jnp.asarray(inv[lane % HALF].reshape(1, LANES), F32)
    sgn_row = jnp.asarray(np.where(lane < HEAD_DIM, -1.0, 1.0).reshape(1, LANES), F32)
    rows = min(T, 2048)
    return pl.pallas_call(
        _rope_kernel,
        grid=(T // rows,),
        in_specs=[pl.BlockSpec((rows, 1), lambda i: (i, 0)),
                  pl.BlockSpec((1, LANES), lambda i: (0, 0)),
                  pl.BlockSpec((1, LANES), lambda i: (0, 0))],
        out_specs=[pl.BlockSpec((rows, LANES), lambda i: (i, 0)),
                   pl.BlockSpec((rows, LANES), lambda i: (i, 0))],
        out_shape=[jax.ShapeDtypeStruct((T, LANES), F32)] * 2,
        compiler_params=_params(("arbitrary",)),
        name="rope_tables",
    )(positions.reshape(T, 1), inv_row, sgn_row)


def _mod_kernel(c_ref, w_ref, b_ref, o_ref):
    ca = _silu(c_ref[...])
    c_hi, c_lo = _split_bf16(ca)
    w_hi, w_lo = _split_bf16(w_ref[...])
    acc = _dot(c_hi, w_hi) + _dot(c_lo, w_hi) + _dot(c_hi, w_lo)
    o_ref[...] = acc + b_ref[...]


def adaln_mod(c, ada_w, ada_b):
    depth, D, D3 = ada_w.shape
    B = c.shape[0]
    cp = jnp.pad(c, ((0, SUBLANES - B), (0, 0)))
    tn = 1024
    return pl.pallas_call(
        _mod_kernel,
        grid=(depth, D3 // tn),
        in_specs=[pl.BlockSpec((SUBLANES, D), lambda l, j: (0, 0)),
                  pl.BlockSpec((None, D, tn), lambda l, j: (l, 0, j)),
                  pl.BlockSpec((None, 1, tn), lambda l, j: (l, 0, j))],
        out_specs=pl.BlockSpec((None, SUBLANES, tn), lambda l, j: (l, 0, j)),
        out_shape=jax.ShapeDtypeStruct((depth, SUBLANES, D3), F32),
        compiler_params=_params(("arbitrary", "arbitrary")),
        name="adaln_mod",
    )(cp, ada_w, ada_b.reshape(depth, 1, D3))


def _w_in_segments(D):
    att = N_GROUPS * ATT_OUT
    seg = {'a_x': (0, D), 'a_g': (D, D)}
    for g in range(N_GROUPS):
        seg[f'q{g}'] = (2 * D + g * ATT_OUT, ATT_OUT)
        seg[f'k{g}'] = (2 * D + att + g * ATT_OUT, ATT_OUT)
        seg[f'v{g}'] = (2 * D + 2 * att + g * ATT_OUT, ATT_OUT)
    off = 2 * D + 3 * att
    for name, width in (('b_g', ATT_OUT), ('c_q', COL), ('c_k', COL), ('c_v', D), ('c_g', D),
                        ('m_g', 3 * D)):
        seg[name] = (off, width)
        off += width
    return seg, off


Z_AX, Z_AG, Z_CV, Z_CG, Z_MG, Z_CQK = 0, 1, 2, 3, 4, 7
Z_TILES = 8


def _proj_main_kernel(x_ref, ng_ref, scale_ref, shift_ref, w_ref, a1_ref, a2_ref, ab_ref,
                      z_ref, la_ref, hb_ref, h_scr):
    @pl.when(pl.program_id(1) == 0)
    def _():
        xf = x_ref[...]
        ms = jnp.mean(xf * xf, axis=-1, keepdims=True)
        y = xf * lax.rsqrt(ms + EPS) * ng_ref[...]
        h = y * (1.0 + scale_ref[...]) + shift_ref[...]
        hb = h.astype(BF16)
        h_scr[...] = hb
        hb_ref[...] = hb
        t = _dot(hb, a1_ref[...]).astype(BF16)
        pre = _dot(t, a2_ref[...]) + ab_ref[...]
        la_ref[...] = -_softplus(-pre) * (1.0 / GLA_NORMALIZER)

    z_ref[...] = _dot(h_scr[...], w_ref[...]).astype(z_ref.dtype)


def proj_main(x2, mod, norm_g, w_in, gla_a1, gla_a2, gla_ab, *, batch, tm):
    T, D = x2.shape
    kw = gla_a2.shape[1]
    tiles_per_batch = T // batch // tm
    seg, total = _w_in_segments(D)
    assert total == w_in.shape[1] and 2 * COL == D
    order = ('a_x', 'a_g', 'c_v', 'c_g', 'm_g', 'c_q', 'c_k')
    w = jnp.concatenate([w_in[:, seg[n][0]:seg[n][0] + seg[n][1]] for n in order], axis=1).astype(BF16)
    a1p = jnp.pad(gla_a1, ((0, 0), (0, LANES - GLA_RANK))).astype(BF16)
    a2p = jnp.pad(gla_a2, ((0, LANES - GLA_RANK), (0, 0))).astype(BF16)
    scale = mod[:, None, D:2 * D]
    shift = mod[:, None, 0:D]
    bmap = lambda i, j: (i // tiles_per_batch, 0, 0)
    return pl.pallas_call(
        _proj_main_kernel,
        grid=(T // tm, Z_TILES),
        in_specs=[pl.BlockSpec((tm, D), lambda i, j: (i, 0)),
                  pl.BlockSpec((1, D), lambda i, j: (0, 0)),
                  pl.BlockSpec((None, 1, D), bmap),
                  pl.BlockSpec((None, 1, D), bmap),
                  pl.BlockSpec((D, D), lambda i, j: (0, j)),
                  pl.BlockSpec((D, LANES), lambda i, j: (0, 0)),
                  pl.BlockSpec((LANES, kw), lambda i, j: (0, 0)),
                  pl.BlockSpec((1, kw), lambda i, j: (0, 0))],
        out_specs=[pl.BlockSpec((tm, D), lambda i, j: (i, j)),
                   pl.BlockSpec((tm, kw), lambda i, j: (i, 0)),
                   pl.BlockSpec((tm, D), lambda i, j: (i, 0))],
        out_shape=[jax.ShapeDtypeStruct((T, Z_TILES * D), BF16),
                   jax.ShapeDtypeStruct((T, kw), F32),
                   jax.ShapeDtypeStruct((T, D), BF16)],
        scratch_shapes=[pltpu.VMEM((tm, D), BF16)],
        compiler_params=_params(("arbitrary", "arbitrary")),
        name="proj_main",
    )(x2, norm_g.reshape(1, D), scale, shift, w, a1p, a2p, gla_ab.reshape(1, kw))


def _proj_att_kernel(h_ref, w_ref, cos_ref, sin_ref, gain_ref, ones_ref,
                     qkv0_ref, qkv1_ref, qkv2_ref, bg_ref, val_scr):
    j = pl.program_id(1)
    tm = h_ref.shape[0]
    nchunk = COL // LANES
    acc = _dot(h_ref[...], w_ref[...])
    for c in range(nchunk):
        val_scr[c] = acc[:, c * LANES:(c + 1) * LANES]
    kind = j % 3
    is_att = j < N_ATT_TILES

    @pl.when(jnp.logical_not(is_att))
    def _():
        for c in range(nchunk):
            bg_ref[:, c * LANES:(c + 1) * LANES] = val_scr[c].astype(bg_ref.dtype)

    @pl.when(jnp.logical_and(is_att, kind < 2))
    def _():
        ones = ones_ref[...]
        qscale = jnp.where(kind == 0, HEAD_DIM ** -0.5, 1.0).astype(F32)
        cos = cos_ref[...]
        sin = sin_ref[...]
        for c2 in range(COL // MXU_DIM):
            sl2 = slice(c2 * MXU_DIM, (c2 + 1) * MXU_DIM)
            a2 = jnp.concatenate([val_scr[2 * c2], val_scr[2 * c2 + 1]], axis=-1)
            ssq = _dot((a2 * a2).astype(BF16), ones)
            xn = a2 * lax.rsqrt(ssq * (1.0 / HEAD_DIM) + EPS) * gain_ref[:, sl2]
            for cc in range(MXU_DIM // LANES):
                xc = xn[:, cc * LANES:(cc + 1) * LANES]
                partner = pltpu.roll(xc, HEAD_DIM, 1)
                val_scr[c2 * (MXU_DIM // LANES) + cc] = (xc * cos + partner * sin) * qscale

    for g, out_ref in enumerate((qkv0_ref, qkv1_ref, qkv2_ref)):
        dil = DIL_PATTERNS[g][1]

        @pl.when(j // 3 == g)
        def _(dil=dil, out_ref=out_ref):
            for c in range(nchunk):
                sl = slice(c * LANES, (c + 1) * LANES)
                for r in range(dil):
                    rows = pl.ds(r, tm // dil, stride=dil) if dil > 1 else slice(None)
                    out_ref[r, :, sl] = val_scr[c, rows, :].astype(out_ref.dtype)


def proj_att(hb, w_in, cos_t, sin_t, qn_g, kn_g, *, batch, tm):
    T, D = hb.shape
    S = T // batch
    tiles_per_batch = S // tm
    seg, _ = _w_in_segments(D)
    cols = []
    for g in range(N_GROUPS):
        for kind in ('q', 'k', 'v'):
            s0, wd = seg[f'{kind}{g}']
            blk = w_in[:, s0:s0 + wd]
            cols.append(blk if kind == 'v' else _pair_permute_cols(blk))
    s0, wd = seg['b_g']
    cols.append(w_in[:, s0:s0 + wd])
    w = jnp.concatenate(cols, axis=1).astype(BF16)
    ntile = N_ATT_TILES + 1
    gains = jnp.ones((N_GROUPS, 3, COL), F32)
    gains = gains.at[:, 0].set(_pair_permute_cols(jnp.tile(qn_g, (1, HEADS_PER_GROUP))))
    gains = gains.at[:, 1].set(_pair_permute_cols(jnp.tile(kn_g, (1, HEADS_PER_GROUP))))
    gains = gains.reshape(N_ATT_TILES, 1, COL)
    lane = np.arange(MXU_DIM)
    head = (lane // LANES) * 2 + (lane % HEAD_DIM) // HALF
    ones_bd = jnp.asarray(head[:, None] == head[None, :], BF16)

    def qkv_spec(g):
        dil = DIL_PATTERNS[g][1]
        assert tm % (dil * BF16_ROWS) == 0
        return pl.BlockSpec(
            (None, None, dil, tm // dil, COL),
            lambda i, j: (i // tiles_per_batch, jnp.clip(j - 3 * g, 0, 2), 0, i % tiles_per_batch, 0))

    def qkv_shape(g):
        dil = DIL_PATTERNS[g][1]
        return jax.ShapeDtypeStruct((batch, 3, dil, S // dil, COL), BF16)

    return pl.pallas_call(
        _proj_att_kernel,
        grid=(T // tm, ntile),
        in_specs=[pl.BlockSpec((tm, D), lambda i, j: (i, 0)),
                  pl.BlockSpec((D, COL), lambda i, j: (0, j)),
                  pl.BlockSpec((tm, LANES), lambda i, j: (i, 0)),
                  pl.BlockSpec((tm, LANES), lambda i, j: (i, 0)),
                  pl.BlockSpec((None, 1, COL), lambda i, j: (jnp.minimum(j, N_ATT_TILES - 1), 0, 0)),
                  pl.BlockSpec((MXU_DIM, MXU_DIM), lambda i, j: (0, 0))],
        out_specs=[qkv_spec(0), qkv_spec(1), qkv_spec(2),
                   pl.BlockSpec((tm, COL), lambda i, j: (i, 0))],
        out_shape=[qkv_shape(0), qkv_shape(1), qkv_shape(2),
                   jax.ShapeDtypeStruct((T, COL), BF16)],
        scratch_shapes=[pltpu.VMEM((COL // LANES, tm, LANES), F32)],
        compiler_params=_params(("arbitrary", "arbitrary")),
        name="proj_att",
    )(hb, w, cos_t, sin_t, gains, ones_bd)


def _lru_kernel(ax_ref, ag_ref, cw_ref, cb_ref, wa_ref, ba_ref, wx_ref, bx_ref, lam_ref, y_ref,
                xe_scr, a_scr, u_scr, hc_scr, *, ts):
    s = pl.program_id(1)
    W = ax_ref.shape[-1]

    @pl.when(s == 0)
    def _():
        xe_scr[0:SUBLANES, :] = jnp.zeros((SUBLANES, W), F32)
        hc_scr[...] = jnp.zeros_like(hc_scr)

    @pl.when(s > 0)
    def _():
        xe_scr[0:SUBLANES, :] = xe_scr[ts:ts + SUBLANES, :]

    xcur = ax_ref[...].astype(F32)
    xe_scr[SUBLANES:SUBLANES + ts, :] = xcur
    xc = cb_ref[...] + xcur * cw_ref[CONV_W - 1:CONV_W, :]
    for k in range(CONV_W - 1):
        off = SUBLANES - (CONV_W - 1) + k
        xc = xc + xe_scr[off:off + ts, :] * cw_ref[k:k + 1, :]

    neg_c_sp = -LRU_C * _softplus(-lam_ref[...])
    for g in range(W // MXU_DIM):
        sl = slice(g * MXU_DIM, (g + 1) * MXU_DIM)
        xg = xc[:, sl]
        xb = xg.astype(BF16)
        r = _sigmoid(_dot(xb, wa_ref[g]) + ba_ref[:, sl])
        i = _sigmoid(_dot(xb, wx_ref[g]) + bx_ref[:, sl])
        log_a = neg_c_sp[:, sl] * r
        a = jnp.exp(log_a)
        a_scr[:, sl] = a
        y = jnp.tanh(-log_a) * (1.0 + a * a)
        mult = jnp.where(y > 0.0, y * lax.rsqrt(y), 0.0)
        u_scr[:, sl] = mult * (i * xg)

    row = lax.broadcasted_iota(jnp.int32, (SUBLANES, W), 0)

    def body(t, h_prev):
        rs = pl.ds(pl.multiple_of(t * SUBLANES, SUBLANES), SUBLANES)
        a = a_scr[rs, :]
        u = u_scr[rs, :]
        d = 1
        while d < SUBLANES:
            keep = row >= d
            a_sh = jnp.where(keep, pltpu.roll(a, d, 0), 1.0)
            u_sh = jnp.where(keep, pltpu.roll(u, d, 0), 0.0)
            u = a * u_sh + u
            a = a * a_sh
            d *= 2
        h = a * h_prev + u
        u_scr[rs, :] = h
        return jnp.broadcast_to(h[SUBLANES - 1:SUBLANES, :], (SUBLANES, W))

    hc_scr[...] = lax.fori_loop(0, ts // SUBLANES, body, hc_scr[...])
    y_ref[...] = (u_scr[...] * _silu(ag_ref[...].astype(F32))).astype(y_ref.dtype)


def lru_branch(z3, conv_w, conv_b, lru_wa, lru_ba, lru_wx, lru_bx, lru_lambda, *, ts):
    B, S, _ = z3.shape
    W = conv_w.shape[1]
    bw = W // LRU_BLOCKS
    per = MXU_DIM // bw

    def block_diag(w):
        w4 = w.reshape(W // MXU_DIM, per, bw, bw)
        eye = jnp.eye(per, dtype=w.dtype)
        return jnp.einsum('gpjk,pq->gpjqk', w4, eye).reshape(W // MXU_DIM, MXU_DIM, MXU_DIM).astype(BF16)

    row = lambda v: v.reshape(1, W)
    full = lambda shape: pl.BlockSpec(shape, lambda b, s: (0,) * len(shape))
    return pl.pallas_call(
        functools.partial(_lru_kernel, ts=ts),
        grid=(B, S // ts),
        in_specs=[pl.BlockSpec((None, ts, W), lambda b, s: (b, s, Z_AX)),
                  pl.BlockSpec((None, ts, W), lambda b, s: (b, s, Z_AG)),
                  full((CONV_W, W)), full((1, W)),
                  full((W // MXU_DIM, MXU_DIM, MXU_DIM)), full((1, W)),
                  full((W // MXU_DIM, MXU_DIM, MXU_DIM)), full((1, W)), full((1, W))],
        out_specs=pl.BlockSpec((None, ts, W), lambda b, s: (b, s, 0)),
        out_shape=jax.ShapeDtypeStruct((B, S, W), BF16),
        scratch_shapes=[pltpu.VMEM((ts + SUBLANES, W), F32), pltpu.VMEM((ts, W), F32),
                        pltpu.VMEM((ts, W), F32), pltpu.VMEM((SUBLANES, W), F32)],
        compiler_params=_params(("arbitrary", "arbitrary")),
        name="lru_branch",
    )(z3, z3, conv_w, row(conv_b), block_diag(lru_wa), row(lru_ba), block_diag(lru_wx),
      row(lru_bx), row(lru_lambda))


def _attn_kernel(q_ref, kp_ref, kc_ref, vp_ref, vc_ref, o_ref, lse_ref, s_scr, p_scr, *, span, tq):
    n = pl.program_id(2)
    blk = BAND_BLK
    nqb = tq // blk
    npair = COL // LANES
    qi = lax.broadcasted_iota(jnp.int32, (blk, 2 * blk), 0)
    kj = lax.broadcasted_iota(jnp.int32, (blk, 2 * blk), 1)
    rel = qi + blk - kj
    band = (rel >= 0) & (rel <= span)
    bias = jnp.where(band, 0.0, NEG_BIG).astype(F32)
    bias_first = jnp.where(band & ((kj >= blk) | (n > 0)), 0.0, NEG_BIG).astype(F32)
    k = jnp.concatenate([kp_ref[...], kc_ref[...]], axis=0)
    v = jnp.concatenate([vp_ref[...], vc_ref[...]], axis=0)
    lane_k = lax.broadcasted_iota(jnp.int32, (blk + tq, LANES), 1)
    lane_q = lax.broadcasted_iota(jnp.int32, (tq, LANES), 1)
    lane = lax.broadcasted_iota(jnp.int32, (blk, LANES), 1)
    lanes_per_head = LANES // HEADS_PER_GROUP
    zero = jnp.zeros((), BF16)
    tile = lambda c, qb, hh: (c * nqb + qb) * 2 + hh

    for c in range(npair):
        sl = slice(c * LANES, (c + 1) * LANES)
        q_c = q_ref[:, sl]
        k_c = k[:, sl]
        q_heads = (jnp.where((lane_q % HEAD_DIM) < HALF, q_c, zero),
                   jnp.where((lane_q % HEAD_DIM) >= HALF, q_c, zero))
        for qb in range(nqb):
            for hh in range(2):
                s = _dot_nt(q_heads[hh][qb * blk:(qb + 1) * blk], k_c[qb * blk:(qb + 2) * blk])
                s_scr[tile(c, qb, hh)] = s + (bias_first if qb == 0 else bias)

    inv_l = {}
    lse_tiles = [jnp.zeros((blk, LANES), F32) for _ in range(nqb)]
    for c in range(npair):
        for qb in range(nqb):
            for hh in range(2):
                t = tile(c, qb, hh)
                s = s_scr[t]
                m = jnp.max(s, axis=-1, keepdims=True)
                p = jnp.exp(s - m)
                l = jnp.sum(p, axis=-1, keepdims=True)
                p_scr[t] = p.astype(BF16)
                inv_l[t] = 1.0 / l
                h = 2 * c + hh
                lse_tiles[qb] = jnp.where(lane // lanes_per_head == h, m + jnp.log(l), lse_tiles[qb])

    for c in range(npair):
        sl = slice(c * LANES, (c + 1) * LANES)
        v_c = v[:, sl]
        v_heads = (jnp.where(lane_k < HEAD_DIM, v_c, zero), jnp.where(lane_k >= HEAD_DIM, v_c, zero))
        for qb in range(nqb):
            ks = slice(qb * blk, (qb + 2) * blk)
            o_pair = (_dot(p_scr[tile(c, qb, 0)], v_heads[0][ks])
                      + _dot(p_scr[tile(c, qb, 1)], v_heads[1][ks]))
            scale = jnp.where(lane < HEAD_DIM, inv_l[tile(c, qb, 0)], inv_l[tile(c, qb, 1)])
            o_ref[qb * blk:(qb + 1) * blk, sl] = (o_pair * scale).astype(o_ref.dtype)
    for qb in range(nqb):
        lse_ref[qb * blk:(qb + 1) * blk, :] = lse_tiles[qb]


def attn_group(qkv, g, window, *, tq):
    B, _, dil, L, _ = qkv.shape
    tq = min(tq, L)
    assert L % tq == 0 and tq % BAND_BLK == 0
    per = tq // BAND_BLK
    cur = lambda kind: pl.BlockSpec((None, None, None, tq, COL), lambda b, r, n: (b, kind, r, n, 0))
    prev = lambda kind: pl.BlockSpec((None, None, None, BAND_BLK, COL),
                                     lambda b, r, n: (b, kind, r, jnp.maximum(n * per - 1, 0), 0))
    return pl.pallas_call(
        functools.partial(_attn_kernel, span=window // dil, tq=tq),
        grid=(B, dil, L // tq),
        in_specs=[cur(0), prev(1), cur(1), prev(2), cur(2)],
        out_specs=[pl.BlockSpec((None, None, tq, COL), lambda b, r, n: (b, r, n, 0)),
                   pl.BlockSpec((None, None, tq, LANES), lambda b, r, n: (b, r, n, 0))],
        out_shape=[jax.ShapeDtypeStruct((B, dil, L, COL), BF16),
                   jax.ShapeDtypeStruct((B, dil, L, LANES), F32)],
        scratch_shapes=[pltpu.VMEM((HEADS_PER_GROUP * per, BAND_BLK, 2 * BAND_BLK), F32),
                        pltpu.VMEM((HEADS_PER_GROUP * per, BAND_BLK, 2 * BAND_BLK), BF16)],
        compiler_params=_params(("arbitrary", "arbitrary", "arbitrary")),
        name=f"attn_group{g}",
    )(qkv, qkv, qkv, qkv, qkv)


def _gla_kernel(q_ref, k_ref, v_ref, g_ref, la_ref, og_ref, y_ref, st_scr, *, ts):
    s = pl.program_id(1)
    C = GLA_CHUNK
    kw = q_ref.shape[-1]
    dk = kw // GLA_HEADS
    dv = v_ref.shape[-1] // GLA_HEADS

    @pl.when(s == 0)
    def _():
        st_scr[...] = jnp.zeros_like(st_scr)

    row = lax.broadcasted_iota(jnp.int32, (C, kw), 0)
    qi = lax.broadcasted_iota(jnp.int32, (C, C), 0)
    kj = lax.broadcasted_iota(jnp.int32, (C, C), 1)
    causal = qi >= kj
    for c in range(ts // C):
        rs = slice(c * C, (c + 1) * C)
        b = la_ref[rs, :]
        d = 1
        while d < C:
            b = b + jnp.where(row >= d, pltpu.roll(b, d, 0), 0.0)
            d *= 2
        b_last = b[C - 1:C, :]
        qf = q_ref[rs, :].astype(F32)
        kf = k_ref[rs, :].astype(F32)
        q_dec = (qf * (dk ** -0.5) * jnp.exp(b)).astype(BF16)
        k_inv = (kf * jnp.exp(-b)).astype(BF16)
        k_end = (kf * jnp.exp(b_last - b)).astype(BF16)
        dec = jnp.exp(b_last)
        for h in range(GLA_HEADS):
            ks = slice(h * dk, (h + 1) * dk)
            vs = slice(h * dv, (h + 1) * dv)
            vh = v_ref[rs, vs]
            att = jnp.where(causal, _dot_nt(q_dec[:, ks], k_inv[:, ks]), 0.0)
            st = st_scr[h]
            o = _dot(att.astype(BF16), vh) + _dot_nt(q_dec[:, ks], st.astype(BF16))
            st_scr[h] = st * dec[:, ks] + _dot_tn(vh, k_end[:, ks])
            ms = jnp.mean(o * o, axis=-1, keepdims=True)
            yn = o * lax.rsqrt(ms + EPS) * og_ref[...]
            y_ref[rs, vs] = (yn * _silu(g_ref[rs, vs].astype(F32))).astype(y_ref.dtype)


def gla_branch(z3, la3, gla_on_g, *, ts):
    B, S, _ = z3.shape
    kw = la3.shape[-1]
    dv = gla_on_g.shape[0]
    vw = GLA_HEADS * dv
    dk = kw // GLA_HEADS
    assert vw == 2 * kw
    zmap = lambda width, col: pl.BlockSpec((None, ts, width), lambda b, s: (b, s, col))
    return pl.pallas_call(
        functools.partial(_gla_kernel, ts=ts),
        grid=(B, S // ts),
        in_specs=[zmap(kw, 2 * Z_CQK), zmap(kw, 2 * Z_CQK + 1), zmap(vw, Z_CV), zmap(vw, Z_CG),
                  pl.BlockSpec((None, ts, kw), lambda b, s: (b, s, 0)),
                  pl.BlockSpec((1, dv), lambda b, s: (0, 0))],
        out_specs=pl.BlockSpec((None, ts, vw), lambda b, s: (b, s, 0)),
        out_shape=jax.ShapeDtypeStruct((B, S, vw), BF16),
        scratch_shapes=[pltpu.VMEM((GLA_HEADS, dv, dk), F32)],
        compiler_params=_params(("arbitrary", "arbitrary")),
        name="gla_branch",
    )(z3, z3, z3, z3, la3, gla_on_g.reshape(1, dv))


def _merge_kernel(x_ref, gate_ref, ya_ref, o0_ref, o1_ref, o2_ref, l0_ref, l1_ref, l2_ref,
                  bg_ref, yc_ref, m0_ref, m1_ref, m2_ref, pa_ref, pb_ref, pc_ref, wo_ref, out_ref,
                  o_scr, l_scr):
    tm = x_ref.shape[0]
    for g, (o_ref, l_ref) in enumerate(((o0_ref, l0_ref), (o1_ref, l1_ref), (o2_ref, l2_ref))):
        dil = DIL_PATTERNS[g][1]
        for r in range(dil):
            rows = pl.ds(r, tm // dil, stride=dil) if dil > 1 else slice(None)
            l_scr[g, rows, :] = l_ref[r]
            for c in range(ATT_OUT // LANES):
                o_scr[g, c, rows, :] = o_ref[r, :, c * LANES:(c + 1) * LANES].astype(F32)
    l0, l1, l2 = l_scr[0], l_scr[1], l_scr[2]
    mx = jnp.maximum(jnp.maximum(l0, l1), l2)
    e0, e1, e2 = jnp.exp(l0 - mx), jnp.exp(l1 - mx), jnp.exp(l2 - mx)
    inv_den = 1.0 / (e0 + e1 + e2)
    wts = (e0 * inv_den, e1 * inv_den, e2 * inv_den)
    lanes_per_head = LANES // HEADS_PER_GROUP
    lane = lax.broadcasted_iota(jnp.int32, (tm, LANES), 1)
    upper = lane >= HEAD_DIM
    cols = []
    for c in range(ATT_OUT // LANES):
        acc = jnp.zeros((tm, LANES), F32)
        for g, w in enumerate(wts):
            h0 = 2 * c * lanes_per_head
            h1 = (2 * c + 1) * lanes_per_head
            wv = jnp.where(upper, w[:, h1:h1 + 1], w[:, h0:h0 + 1])
            acc = acc + wv * o_scr[g, c]
        cols.append(acc)
    yb = jnp.concatenate(cols, axis=-1) * _silu(bg_ref[...].astype(F32))
    merged = (_sigmoid(m0_ref[...].astype(F32)) * _dot(ya_ref[...], pa_ref[...])
              + _sigmoid(m1_ref[...].astype(F32)) * _dot(yb.astype(BF16), pb_ref[...])
              + _sigmoid(m2_ref[...].astype(F32)) * _dot(yc_ref[...], pc_ref[...]))
    out_ref[...] = x_ref[...] + gate_ref[...] * _dot(merged.astype(BF16), wo_ref[...])


def merge_layer(x2, mod, ya, o_groups, lse_groups, bg, yc, z2, proj_a, proj_b, proj_c, w_o,
                *, batch, tm):
    T, D = x2.shape
    tiles_per_batch = T // batch // tm
    gate = mod[:, None, 2 * D:3 * D]
    rows = lambda width, col=0: pl.BlockSpec((tm, width), lambda i: (i, col))
    full = lambda a: pl.BlockSpec(a.shape, lambda i: (0, 0))

    def residue_major(g, width):
        dil = DIL_PATTERNS[g][1]
        assert tm % (dil * BF16_ROWS) == 0
        return pl.BlockSpec((None, dil, tm // dil, width),
                            lambda i: (i // tiles_per_batch, 0, i % tiles_per_batch, 0))

    weights = [w.astype(BF16) for w in (proj_a, proj_b, proj_c, w_o)]
    return pl.pallas_call(
        _merge_kernel,
        grid=(T // tm,),
        in_specs=[rows(D), pl.BlockSpec((None, 1, D), lambda i: (i // tiles_per_batch, 0, 0)),
                  rows(D)] + [residue_major(g, COL) for g in range(N_GROUPS)]
                 + [residue_major(g, LANES) for g in range(N_GROUPS)]
                 + [rows(COL), rows(D), rows(D, Z_MG), rows(D, Z_MG + 1), rows(D, Z_MG + 2)]
                 + [full(w) for w in weights],
        out_specs=rows(D),
        out_shape=jax.ShapeDtypeStruct((T, D), F32),
        scratch_shapes=[pltpu.VMEM((N_GROUPS, ATT_OUT // LANES, tm, LANES), F32),
                        pltpu.VMEM((N_GROUPS, tm, LANES), F32)],
        compiler_params=_params(("arbitrary",)),
        name="merge_layer",
    )(x2, gate, ya, *o_groups, *lse_groups, bg, yc, z2, z2, z2, *weights)


def kernel(x, c, positions, ada_w, ada_b, norm_g, w_in, conv_w, conv_b, lru_wa, lru_ba, lru_wx,
           lru_bx, lru_lambda, qn_g, kn_g, gla_a1, gla_a2, gla_ab, gla_on_g, proj_a, proj_b,
           proj_c, w_o):
    B, S, D = x.shape
    T = B * S
    depth = ada_w.shape[0]
    tm_proj = min(1024, S)
    tm_merge = min(512, S)
    ts_seq = min(512, S)
    tq_att = 512
    cos_t, sin_t = rope_tables(positions)
    mods = adaln_mod(c, ada_w, ada_b)
    x2 = x.reshape(T, D)
    for l in range(depth):
        z2, la, hb = proj_main(x2, mods[l], norm_g[l], w_in[l], gla_a1[l], gla_a2[l], gla_ab[l],
                               batch=B, tm=tm_proj)
        *qkv, bg = proj_att(hb, w_in[l], cos_t, sin_t, qn_g[l], kn_g[l], batch=B, tm=tm_proj)
        z3 = z2.reshape(B, S, -1)
        ya = lru_branch(z3, conv_w[l], conv_b[l], lru_wa[l], lru_ba[l], lru_wx[l], lru_bx[l],
                        lru_lambda[l], ts=ts_seq)
        outs = [attn_group(qkv[g], g, win, tq=tq_att) for g, (win, _) in enumerate(DIL_PATTERNS)]
        yc = gla_branch(z3, la.reshape(B, S, -1), gla_on_g[l], ts=ts_seq)
        x2 = merge_layer(x2, mods[l], ya.reshape(T, -1), [o for o, _ in outs],
                         [s for _, s in outs], bg, yc.reshape(T, -1), z2, proj_a[l], proj_b[l],
                         proj_c[l], w_o[l], batch=B, tm=tm_merge)
    return x2.reshape(B, S, D)
```

```python
import functools

import jax
import jax.numpy as jnp
import numpy as np
from jax import lax
from jax.experimental import pallas as pl
from jax.experimental.pallas import tpu as pltpu

LRU_BLOCKS = 16
CONV_W = 4
LRU_C = 8.0
DIL_PATTERNS = ((128, 1), (512, 4), (2048, 16))
N_GROUPS = 3
HEADS_PER_GROUP = 8
HEAD_DIM = 64
HALF = HEAD_DIM // 2
ATT_OUT = HEADS_PER_GROUP * HEAD_DIM
BAND_BLK = 128
ROPE_THETA = 10000.0
GLA_HEADS = 4
GLA_RANK = 16
GLA_NORMALIZER = 16.0
GLA_CHUNK = 64
EPS = 1e-6

LANES = 128
SUBLANES = 8
BF16_ROWS = 16
MXU_DIM = 256
VMEM_LIMIT = 48 * 1024 * 1024

COL = 512
N_ATT_TILES = 3 * N_GROUPS
NEG_BIG = -1e30
LOG2E = float(np.log2(np.e))
LN2 = float(np.log(2.0))
BF16 = jnp.bfloat16
F32 = jnp.float32

_PAIR_PERM = np.concatenate([np.arange(0, HALF), np.arange(HEAD_DIM, HEAD_DIM + HALF),
                             np.arange(HALF, HEAD_DIM), np.arange(HEAD_DIM + HALF, 2 * HEAD_DIM)])


def _params(sem):
    return pltpu.CompilerParams(dimension_semantics=sem, vmem_limit_bytes=VMEM_LIMIT)


def _split_bf16(x):
    hi = x.astype(BF16)
    lo = (x - hi.astype(F32)).astype(BF16)
    return hi, lo


def _dot(a, b):
    return jnp.dot(a, b, preferred_element_type=F32)


def _dot_nt(a, b):
    return lax.dot_general(a, b, (((1,), (1,)), ((), ())), preferred_element_type=F32)


def _dot_tn(a, b):
    return lax.dot_general(a, b, (((0,), (0,)), ((), ())), preferred_element_type=F32)


def _sigmoid(x):
    return 0.5 * jnp.tanh(0.5 * x) + 0.5


def _silu(x):
    h = 0.5 * x
    return h + h * jnp.tanh(h)


def _softplus(x):
    return jnp.maximum(x, 0.0) + jnp.log1p(jnp.exp(-jnp.abs(x)))


def _pair_permute_cols(w):
    shp = w.shape
    return w.reshape(shp[:-1] + (shp[-1] // LANES, LANES))[..., _PAIR_PERM].reshape(shp)


def _rope_kernel(pos_ref, inv_ref, sgn_ref, cos_ref, sin_ref):
    ang = pos_ref[...].astype(F32) * inv_ref[...]
    cos_ref[...] = jnp.cos(ang)
    sin_ref[...] = jnp.sin(ang) * sgn_ref[...]


def rope_tables(positions):
    T = positions.size
    inv = ROPE_THETA ** (-(np.arange(HALF, dtype=np.float32) / HALF))
    lane = np.arange(LANES)
    inv_row = jnp.asarray(inv[lane % HALF].reshape(1, LANES), F32)
    sgn_row = jnp.asarray(np.where(lane < HEAD_DIM, -1.0, 1.0).reshape(1, LANES), F32)
    rows = min(T, 2048)
    return pl.pallas_call(
        _rope_kernel,
        grid=(T // rows,),
        in_specs=[pl.BlockSpec((rows, 1), lambda i: (i, 0)),
                  pl.BlockSpec((1, LANES), lambda i: (0, 0)),
                  pl.BlockSpec((1, LANES), lambda i: (0, 0))],
        out_specs=[pl.BlockSpec((rows, LANES), lambda i: (i, 0)),
                   pl.BlockSpec((rows, LANES), lambda i: (i, 0))],
        out_shape=[jax.ShapeDtypeStruct((T, LANES), F32)] * 2,
        compiler_params=_params(("arbitrary",)),
        name="rope_tables",
    )(positions.reshape(T, 1), inv_row, sgn_row)


def _mod_kernel(c_ref, w_ref, b_ref, o_ref):
    ca = _silu(c_ref[...])
    c_hi, c_lo = _split_bf16(ca)
    w_hi, w_lo = _split_bf16(w_ref[...])
    acc = _dot(c_hi, w_hi) + _dot(c_lo, w_hi) + _dot(c_hi, w_lo)
    o_ref[...] = acc + b_ref[...]


def adaln_mod(c, ada_w, ada_b):
    depth, D, D3 = ada_w.shape
    B = c.shape[0]
    cp = jnp.pad(c, ((0, SUBLANES - B), (0, 0)))
    tn = 1024
    return pl.pallas_call(
        _mod_kernel,
        grid=(depth, D3 // tn),
        in_specs=[pl.BlockSpec((SUBLANES, D), lambda l, j: (0, 0)),
                  pl.BlockSpec((None, D, tn), lambda l, j: (l, 0, j)),
                  pl.BlockSpec((None, 1, tn), lambda l, j: (l, 0, j))],
        out_specs=pl.BlockSpec((None, SUBLANES, tn), lambda l, j: (l, 0, j)),
        out_shape=jax.ShapeDtypeStruct((depth, SUBLANES, D3), F32),
        compiler_params=_params(("arbitrary", "arbitrary")),
        name="adaln_mod",
    )(cp, ada_w, ada_b.reshape(depth, 1, D3))


def _w_in_segments(D):
    att = N_GROUPS * ATT_OUT
    seg = {'a_x': (0, D), 'a_g': (D, D)}
    for g in range(N_GROUPS):
        seg[f'q{g}'] = (2 * D + g * ATT_OUT, ATT_OUT)
        seg[f'k{g}'] = (2 * D + att + g * ATT_OUT, ATT_OUT)
        seg[f'v{g}'] = (2 * D + 2 * att + g * ATT_OUT, ATT_OUT)
    off = 2 * D + 3 * att
    for name, width in (('b_g', ATT_OUT), ('c_q', COL), ('c_k', COL), ('c_v', D), ('c_g', D),
                        ('m_g', 3 * D)):
        seg[name] = (off, width)
        off += width
    return seg, off


Z_AX, Z_AG, Z_CV, Z_CG, Z_MG, Z_CQK = 0, 1, 2, 3, 4, 7
Z_TILES = 8


def _proj_main_kernel(x_ref, ng_ref, scale_ref, shift_ref, w_ref, a1_ref, a2_ref, ab_ref,
                      z_ref, la_ref, hb_ref, h_scr):
    @pl.when(pl.program_id(1) == 0)
    def _():
        xf = x_ref[...]
        ms = jnp.mean(xf * xf, axis=-1, keepdims=True)
        y = xf * lax.rsqrt(ms + EPS) * ng_ref[...]
        h = y * (1.0 + scale_ref[...]) + shift_ref[...]
        hb = h.astype(BF16)
        h_scr[...] = hb
        hb_ref[...] = hb
        t = _dot(hb, a1_ref[...]).astype(BF16)
        pre = _dot(t, a2_ref[...]) + ab_ref[...]
        la_ref[...] = -_softplus(-pre) * (1.0 / GLA_NORMALIZER)

    z_ref[...] = _dot(h_scr[...], w_ref[...]).astype(z_ref.dtype)


def proj_main(x2, mod, norm_g, w_in, gla_a1, gla_a2, gla_ab, *, batch, tm):
    T, D = x2.shape
    kw = gla_a2.shape[1]
    tiles_per_batch = T // batch // tm
    seg, total = _w_in_segments(D)
    assert total == w_in.shape[1] and 2 * COL == D
    order = ('a_x', 'a_g', 'c_v', 'c_g', 'm_g', 'c_q', 'c_k')
    w = jnp.concatenate([w_in[:, seg[n][0]:seg[n][0] + seg[n][1]] for n in order], axis=1).astype(BF16)
    a1p = jnp.pad(gla_a1, ((0, 0), (0, LANES - GLA_RANK))).astype(BF16)
    a2p = jnp.pad(gla_a2, ((0, LANES - GLA_RANK), (0, 0))).astype(BF16)
    scale = mod[:, None, D:2 * D]
    shift = mod[:, None, 0:D]
    bmap = lambda i, j: (i // tiles_per_batch, 0, 0)
    tn = 2 * D
    return pl.pallas_call(
        _proj_main_kernel,
        grid=(T // tm, Z_TILES * D // tn),
        in_specs=[pl.BlockSpec((tm, D), lambda i, j: (i, 0)),
                  pl.BlockSpec((1, D), lambda i, j: (0, 0)),
                  pl.BlockSpec((None, 1, D), bmap),
                  pl.BlockSpec((None, 1, D), bmap),
                  pl.BlockSpec((D, tn), lambda i, j: (0, j)),
                  pl.BlockSpec((D, LANES), lambda i, j: (0, 0)),
                  pl.BlockSpec((LANES, kw), lambda i, j: (0, 0)),
                  pl.BlockSpec((1, kw), lambda i, j: (0, 0))],
        out_specs=[pl.BlockSpec((tm, tn), lambda i, j: (i, j)),
                   pl.BlockSpec((tm, kw), lambda i, j: (i, 0)),
                   pl.BlockSpec((tm, D), lambda i, j: (i, 0))],
        out_shape=[jax.ShapeDtypeStruct((T, Z_TILES * D), BF16),
                   jax.ShapeDtypeStruct((T, kw), F32),
                   jax.ShapeDtypeStruct((T, D), BF16)],
        scratch_shapes=[pltpu.VMEM((tm, D), BF16)],
        compiler_params=_params(("arbitrary", "arbitrary")),
        name="proj_main",
    )(x2, norm_g.reshape(1, D), scale, shift, w, a1p, a2p, gla_ab.reshape(1, kw))


def _proj_att_kernel(h_ref, w_ref, cos_ref, sin_ref, gain_ref, ones_ref,
                     qkv0_ref, qkv1_ref, qkv2_ref, bg_ref, val_scr):
    tm = h_ref.shape[0]
    nchunk = COL // LANES
    h = h_ref[...]
    ones = ones_ref[...]
    cos = cos_ref[...]
    sin = sin_ref[...]
    bg_ref[...] = _dot(h, w_ref[:, N_ATT_TILES * COL:]).astype(bg_ref.dtype)
    for g, out_ref in enumerate((qkv0_ref, qkv1_ref, qkv2_ref)):
        dil = DIL_PATTERNS[g][1]
        for kind in range(3):
            t = 3 * g + kind
            acc = _dot(h, w_ref[:, t * COL:(t + 1) * COL])
            if kind == 2:
                chunks = [acc[:, c * LANES:(c + 1) * LANES] for c in range(nchunk)]
            else:
                qscale = HEAD_DIM ** -0.5 * LOG2E if kind == 0 else 1.0
                chunks = []
                for c2 in range(COL // MXU_DIM):
                    sl2 = slice(c2 * MXU_DIM, (c2 + 1) * MXU_DIM)
                    a2 = acc[:, sl2]
                    ssq = _dot((a2 * a2).astype(BF16), ones)
                    xn = a2 * lax.rsqrt(ssq * (1.0 / HEAD_DIM) + EPS) * (gain_ref[t] * qscale)[:, sl2]
                    for cc in range(MXU_DIM // LANES):
                        xc = xn[:, cc * LANES:(cc + 1) * LANES]
                        chunks.append(xc * cos + pltpu.roll(xc, HEAD_DIM, 1) * sin)
            for c, val in enumerate(chunks):
                sl = slice(c * LANES, (c + 1) * LANES)
                if dil == 1:
                    out_ref[kind, 0, :, sl] = val.astype(out_ref.dtype)
                else:
                    slot = (t * nchunk + c) % val_scr.shape[0]
                    val_scr[slot] = val
                    for r in range(dil):
                        out_ref[kind, r, :, sl] = (
                            val_scr[slot, pl.ds(r, tm // dil, stride=dil), :].astype(out_ref.dtype))


def proj_att(hb, w_in, cos_t, sin_t, qn_g, kn_g, *, batch, tm):
    T, D = hb.shape
    S = T // batch
    tiles_per_batch = S // tm
    seg, _ = _w_in_segments(D)
    cols = []
    for g in range(N_GROUPS):
        for kind in ('q', 'k', 'v'):
            s0, wd = seg[f'{kind}{g}']
            blk = w_in[:, s0:s0 + wd]
            cols.append(blk if kind == 'v' else _pair_permute_cols(blk))
    s0, wd = seg['b_g']
    cols.append(w_in[:, s0:s0 + wd])
    w = jnp.concatenate(cols, axis=1).astype(BF16)
    ntile = N_ATT_TILES + 1
    gains = jnp.ones((N_GROUPS, 3, COL), F32)
    gains = gains.at[:, 0].set(_pair_permute_cols(jnp.tile(qn_g, (1, HEADS_PER_GROUP))))
    gains = gains.at[:, 1].set(_pair_permute_cols(jnp.tile(kn_g, (1, HEADS_PER_GROUP))))
    gains = gains.reshape(N_ATT_TILES, 1, COL)
    lane = np.arange(MXU_DIM)
    head = (lane // LANES) * 2 + (lane % HEAD_DIM) // HALF
    ones_bd = jnp.asarray(head[:, None] == head[None, :], BF16)

    def qkv_spec(g):
        dil = DIL_PATTERNS[g][1]
        assert tm % (dil * BF16_ROWS) == 0
        return pl.BlockSpec((None, 3, dil, tm // dil, COL),
                            lambda i: (i // tiles_per_batch, 0, 0, i % tiles_per_batch, 0))

    def qkv_shape(g):
        dil = DIL_PATTERNS[g][1]
        return jax.ShapeDtypeStruct((batch, 3, dil, S // dil, COL), BF16)

    const = lambda shape: pl.BlockSpec(shape, lambda i: (0,) * len(shape),
                                       pipeline_mode=pl.Buffered(1))
    return pl.pallas_call(
        _proj_att_kernel,
        grid=(T // tm,),
        in_specs=[pl.BlockSpec((tm, D), lambda i: (i, 0)),
                  const((D, ntile * COL)),
                  pl.BlockSpec((tm, LANES), lambda i: (i, 0)),
                  pl.BlockSpec((tm, LANES), lambda i: (i, 0)),
                  const((N_ATT_TILES, 1, COL)),
                  const((MXU_DIM, MXU_DIM))],
        out_specs=[qkv_spec(0), qkv_spec(1), qkv_spec(2),
                   pl.BlockSpec((tm, COL), lambda i: (i, 0))],
        out_shape=[qkv_shape(0), qkv_shape(1), qkv_shape(2),
                   jax.ShapeDtypeStruct((T, COL), BF16)],
        scratch_shapes=[pltpu.VMEM((2 * (COL // LANES), tm, LANES), F32)],
        compiler_params=_params(("arbitrary",)),
        name="proj_att",
    )(hb, w, cos_t, sin_t, gains, ones_bd)


def _lru_kernel(ax_ref, ag_ref, cw_ref, cb_ref, wa_ref, ba_ref, wx_ref, bx_ref, lam_ref, y_ref,
                xe_scr, a_scr, u_scr, hc_scr, *, ts):
    s = pl.program_id(1)
    W = ax_ref.shape[-1]

    @pl.when(s == 0)
    def _():
        xe_scr[0:SUBLANES, :] = jnp.zeros((SUBLANES, W), F32)
        hc_scr[...] = jnp.zeros_like(hc_scr)

    @pl.when(s > 0)
    def _():
        xe_scr[0:SUBLANES, :] = xe_scr[ts:ts + SUBLANES, :]

    xcur = ax_ref[...].astype(F32)
    xe_scr[SUBLANES:SUBLANES + ts, :] = xcur
    xc = cb_ref[...] + xcur * cw_ref[CONV_W - 1:CONV_W, :]
    for k in range(CONV_W - 1):
        off = SUBLANES - (CONV_W - 1) + k
        xc = xc + xe_scr[off:off + ts, :] * cw_ref[k:k + 1, :]

    neg_c_sp = -LRU_C * _softplus(-lam_ref[...])
    for g in range(W // MXU_DIM):
        sl = slice(g * MXU_DIM, (g + 1) * MXU_DIM)
        xg = xc[:, sl]
        xb = xg.astype(BF16)
        r = _sigmoid(_dot(xb, wa_ref[g]) + ba_ref[:, sl])
        i = _sigmoid(_dot(xb, wx_ref[g]) + bx_ref[:, sl])
        log_a = neg_c_sp[:, sl] * r
        a = jnp.exp(log_a)
        a_scr[:, sl] = a
        y = jnp.tanh(-log_a) * (1.0 + a * a)
        mult = jnp.where(y > 0.0, y * lax.rsqrt(y), 0.0)
        u_scr[:, sl] = mult * (i * xg)

    row = lax.broadcasted_iota(jnp.int32, (SUBLANES, W), 0)

    def body(t, h_prev):
        rs = pl.ds(pl.multiple_of(t * SUBLANES, SUBLANES), SUBLANES)
        a = a_scr[rs, :]
        u = u_scr[rs, :]
        d = 1
        while d < SUBLANES:
            keep = row >= d
            a_sh = jnp.where(keep, pltpu.roll(a, d, 0), 1.0)
            u_sh = jnp.where(keep, pltpu.roll(u, d, 0), 0.0)
            u = a * u_sh + u
            a = a * a_sh
            d *= 2
        h = a * h_prev + u
        u_scr[rs, :] = h
        return jnp.broadcast_to(h[SUBLANES - 1:SUBLANES, :], (SUBLANES, W))

    hc_scr[...] = lax.fori_loop(0, ts // SUBLANES, body, hc_scr[...])
    y_ref[...] = (u_scr[...] * _silu(ag_ref[...].astype(F32))).astype(y_ref.dtype)


def lru_branch(z3, conv_w, conv_b, lru_wa, lru_ba, lru_wx, lru_bx, lru_lambda, *, ts):
    B, S, _ = z3.shape
    W = conv_w.shape[1]
    bw = W // LRU_BLOCKS
    per = MXU_DIM // bw

    def block_diag(w):
        w4 = w.reshape(W // MXU_DIM, per, bw, bw)
        eye = jnp.eye(per, dtype=w.dtype)
        return jnp.einsum('gpjk,pq->gpjqk', w4, eye).reshape(W // MXU_DIM, MXU_DIM, MXU_DIM).astype(BF16)

    row = lambda v: v.reshape(1, W)
    full = lambda shape: pl.BlockSpec(shape, lambda b, s: (0,) * len(shape))
    return pl.pallas_call(
        functools.partial(_lru_kernel, ts=ts),
        grid=(B, S // ts),
        in_specs=[pl.BlockSpec((None, ts, W), lambda b, s: (b, s, Z_AX)),
                  pl.BlockSpec((None, ts, W), lambda b, s: (b, s, Z_AG)),
                  full((CONV_W, W)), full((1, W)),
                  full((W // MXU_DIM, MXU_DIM, MXU_DIM)), full((1, W)),
                  full((W // MXU_DIM, MXU_DIM, MXU_DIM)), full((1, W)), full((1, W))],
        out_specs=pl.BlockSpec((None, ts, W), lambda b, s: (b, s, 0)),
        out_shape=jax.ShapeDtypeStruct((B, S, W), BF16),
        scratch_shapes=[pltpu.VMEM((ts + SUBLANES, W), F32), pltpu.VMEM((ts, W), F32),
                        pltpu.VMEM((ts, W), F32), pltpu.VMEM((SUBLANES, W), F32)],
        compiler_params=_params(("arbitrary", "arbitrary")),
        name="lru_branch",
    )(z3, z3, conv_w, row(conv_b), block_diag(lru_wa), row(lru_ba), block_diag(lru_wx),
      row(lru_bx), row(lru_lambda))


def _attn_kernel(q_ref, kp_ref, kc_ref, vp_ref, vc_ref, o_ref, lse_ref, s_scr, p_scr, *, span, tq):
    n = pl.program_id(2)
    blk = BAND_BLK
    nqb = tq // blk
    npair = COL // LANES
    qi = lax.broadcasted_iota(jnp.int32, (blk, 2 * blk), 0)
    kj = lax.broadcasted_iota(jnp.int32, (blk, 2 * blk), 1)
    rel = qi + blk - kj
    band = (rel >= 0) & (rel <= span)
    bias = jnp.where(band, 0.0, NEG_BIG).astype(F32)
    bias_first = jnp.where(band & ((kj >= blk) | (n > 0)), 0.0, NEG_BIG).astype(F32)
    k = jnp.concatenate([kp_ref[...], kc_ref[...]], axis=0)
    v = jnp.concatenate([vp_ref[...], vc_ref[...]], axis=0)
    lane_k = lax.broadcasted_iota(jnp.int32, (blk + tq, LANES), 1)
    lane_q = lax.broadcasted_iota(jnp.int32, (tq, LANES), 1)
    lane = lax.broadcasted_iota(jnp.int32, (blk, LANES), 1)
    lanes_per_head = LANES // HEADS_PER_GROUP
    zero = jnp.zeros((), BF16)
    tile = lambda c, qb, hh: (c * nqb + qb) * 2 + hh

    for c in range(npair):
        sl = slice(c * LANES, (c + 1) * LANES)
        q_c = q_ref[:, sl]
        k_c = k[:, sl]
        q_heads = (jnp.where((lane_q % HEAD_DIM) < HALF, q_c, zero),
                   jnp.where((lane_q % HEAD_DIM) >= HALF, q_c, zero))
        for qb in range(nqb):
            for hh in range(2):
                s = _dot_nt(q_heads[hh][qb * blk:(qb + 1) * blk], k_c[qb * blk:(qb + 2) * blk])
                s_scr[tile(c, qb, hh)] = s + (bias_first if qb == 0 else bias)

    inv_l = {}
    lse_tiles = [jnp.zeros((blk, LANES), F32) for _ in range(nqb)]
    for c in range(npair):
        for qb in range(nqb):
            for hh in range(2):
                t = tile(c, qb, hh)
                s = s_scr[t]
                m = jnp.max(s, axis=-1, keepdims=True)
                p = jnp.exp2(s - m)
                l = jnp.sum(p, axis=-1, keepdims=True)
                p_scr[t] = p.astype(BF16)
                inv_l[t] = 1.0 / l
                h = 2 * c + hh
                lse = m * LN2 + jnp.log(l)
                lse_tiles[qb] = jnp.where(lane // lanes_per_head == h, lse, lse_tiles[qb])

    for c in range(npair):
        sl = slice(c * LANES, (c + 1) * LANES)
        v_c = v[:, sl]
        v_heads = (jnp.where(lane_k < HEAD_DIM, v_c, zero), jnp.where(lane_k >= HEAD_DIM, v_c, zero))
        for qb in range(nqb):
            ks = slice(qb * blk, (qb + 2) * blk)
            o_pair = (_dot(p_scr[tile(c, qb, 0)], v_heads[0][ks])
                      + _dot(p_scr[tile(c, qb, 1)], v_heads[1][ks]))
            scale = jnp.where(lane < HEAD_DIM, inv_l[tile(c, qb, 0)], inv_l[tile(c, qb, 1)])
            o_ref[qb * blk:(qb + 1) * blk, sl] = (o_pair * scale).astype(o_ref.dtype)
    for qb in range(nqb):
        lse_ref[qb * blk:(qb + 1) * blk, :] = lse_tiles[qb]


def attn_group(qkv, g, window, *, tq):
    B, _, dil, L, _ = qkv.shape
    tq = min(tq, L)
    assert L % tq == 0 and tq % BAND_BLK == 0
    per = tq // BAND_BLK
    cur = lambda kind: pl.BlockSpec((None, None, None, tq, COL), lambda b, r, n: (b, kind, r, n, 0))
    prev = lambda kind: pl.BlockSpec((None, None, None, BAND_BLK, COL),
                                     lambda b, r, n: (b, kind, r, jnp.maximum(n * per - 1, 0), 0))
    return pl.pallas_call(
        functools.partial(_attn_kernel, span=window // dil, tq=tq),
        grid=(B, dil, L // tq),
        in_specs=[cur(0), prev(1), cur(1), prev(2), cur(2)],
        out_specs=[pl.BlockSpec((None, None, tq, COL), lambda b, r, n: (b, r, n, 0)),
                   pl.BlockSpec((None, None, tq, LANES), lambda b, r, n: (b, r, n, 0))],
        out_shape=[jax.ShapeDtypeStruct((B, dil, L, COL), BF16),
                   jax.ShapeDtypeStruct((B, dil, L, LANES), F32)],
        scratch_shapes=[pltpu.VMEM((HEADS_PER_GROUP * per, BAND_BLK, 2 * BAND_BLK), F32),
                        pltpu.VMEM((HEADS_PER_GROUP * per, BAND_BLK, 2 * BAND_BLK), BF16)],
        compiler_params=_params(("arbitrary", "arbitrary", "arbitrary")),
        name=f"attn_group{g}",
    )(qkv, qkv, qkv, qkv, qkv)


def _gla_kernel(q_ref, k_ref, v_ref, g_ref, la_ref, og_ref, y_ref, st_scr, *, ts):
    s = pl.program_id(1)
    C = GLA_CHUNK
    kw = q_ref.shape[-1]
    dk = kw // GLA_HEADS
    dv = v_ref.shape[-1] // GLA_HEADS

    @pl.when(s == 0)
    def _():
        st_scr[...] = jnp.zeros_like(st_scr)

    row = lax.broadcasted_iota(jnp.int32, (C, kw), 0)
    qi = lax.broadcasted_iota(jnp.int32, (C, C), 0)
    kj = lax.broadcasted_iota(jnp.int32, (C, C), 1)
    causal = qi >= kj
    for c in range(ts // C):
        rs = slice(c * C, (c + 1) * C)
        b = la_ref[rs, :]
        d = 1
        while d < C:
            b = b + jnp.where(row >= d, pltpu.roll(b, d, 0), 0.0)
            d *= 2
        b_last = b[C - 1:C, :]
        qf = q_ref[rs, :].astype(F32)
        kf = k_ref[rs, :].astype(F32)
        q_dec = (qf * (dk ** -0.5) * jnp.exp(b)).astype(BF16)
        k_inv = (kf * jnp.exp(-b)).astype(BF16)
        k_end = (kf * jnp.exp(b_last - b)).astype(BF16)
        dec = jnp.exp(b_last)
        for h in range(GLA_HEADS):
            ks = slice(h * dk, (h + 1) * dk)
            vs = slice(h * dv, (h + 1) * dv)
            vh = v_ref[rs, vs]
            att = jnp.where(causal, _dot_nt(q_dec[:, ks], k_inv[:, ks]), 0.0)
            st = st_scr[h]
            o = _dot(att.astype(BF16), vh) + _dot_nt(q_dec[:, ks], st.astype(BF16))
            st_scr[h] = st * dec[:, ks] + _dot_tn(vh, k_end[:, ks])
            ms = jnp.mean(o * o, axis=-1, keepdims=True)
            yn = o * lax.rsqrt(ms + EPS) * og_ref[...]
            y_ref[rs, vs] = (yn * _silu(g_ref[rs, vs].astype(F32))).astype(y_ref.dtype)


def gla_branch(z3, la3, gla_on_g, *, ts):
    B, S, _ = z3.shape
    kw = la3.shape[-1]
    dv = gla_on_g.shape[0]
    vw = GLA_HEADS * dv
    dk = kw // GLA_HEADS
    assert vw == 2 * kw
    zmap = lambda width, col: pl.BlockSpec((None, ts, width), lambda b, s: (b, s, col))
    return pl.pallas_call(
        functools.partial(_gla_kernel, ts=ts),
        grid=(B, S // ts),
        in_specs=[zmap(kw, 2 * Z_CQK), zmap(kw, 2 * Z_CQK + 1), zmap(vw, Z_CV), zmap(vw, Z_CG),
                  pl.BlockSpec((None, ts, kw), lambda b, s: (b, s, 0)),
                  pl.BlockSpec((1, dv), lambda b, s: (0, 0))],
        out_specs=pl.BlockSpec((None, ts, vw), lambda b, s: (b, s, 0)),
        out_shape=jax.ShapeDtypeStruct((B, S, vw), BF16),
        scratch_shapes=[pltpu.VMEM((GLA_HEADS, dv, dk), F32)],
        compiler_params=_params(("arbitrary", "arbitrary")),
        name="gla_branch",
    )(z3, z3, z3, z3, la3, gla_on_g.reshape(1, dv))


def _merge_kernel(x_ref, gate_ref, ya_ref, o0_ref, o1_ref, o2_ref, l0_ref, l1_ref, l2_ref,
                  bg_ref, yc_ref, m0_ref, m1_ref, m2_ref, pa_ref, pb_ref, pc_ref, wo_ref, out_ref,
                  o_scr, l_scr):
    tm = x_ref.shape[0]
    for g, (o_ref, l_ref) in enumerate(((o0_ref, l0_ref), (o1_ref, l1_ref), (o2_ref, l2_ref))):
        dil = DIL_PATTERNS[g][1]
        for r in range(dil):
            rows = pl.ds(r, tm // dil, stride=dil) if dil > 1 else slice(None)
            l_scr[g, rows, :] = l_ref[r]
            for c in range(ATT_OUT // LANES):
                o_scr[g, c, rows, :] = o_ref[r, :, c * LANES:(c + 1) * LANES].astype(F32)
    l0, l1, l2 = l_scr[0], l_scr[1], l_scr[2]
    mx = jnp.maximum(jnp.maximum(l0, l1), l2)
    e0, e1, e2 = jnp.exp(l0 - mx), jnp.exp(l1 - mx), jnp.exp(l2 - mx)
    inv_den = 1.0 / (e0 + e1 + e2)
    wts = (e0 * inv_den, e1 * inv_den, e2 * inv_den)
    lanes_per_head = LANES // HEADS_PER_GROUP
    lane = lax.broadcasted_iota(jnp.int32, (tm, LANES), 1)
    upper = lane >= HEAD_DIM
    cols = []
    for c in range(ATT_OUT // LANES):
        acc = jnp.zeros((tm, LANES), F32)
        for g, w in enumerate(wts):
            h0 = 2 * c * lanes_per_head
            h1 = (2 * c + 1) * lanes_per_head
            wv = jnp.where(upper, w[:, h1:h1 + 1], w[:, h0:h0 + 1])
            acc = acc + wv * o_scr[g, c]
        cols.append(acc)
    yb = jnp.concatenate(cols, axis=-1) * _silu(bg_ref[...].astype(F32))
    merged = (_sigmoid(m0_ref[...].astype(F32)) * _dot(ya_ref[...], pa_ref[...])
              + _sigmoid(m1_ref[...].astype(F32)) * _dot(yb.astype(BF16), pb_ref[...])
              + _sigmoid(m2_ref[...].astype(F32)) * _dot(yc_ref[...], pc_ref[...]))
    out_ref[...] = x_ref[...] + gate_ref[...] * _dot(merged.astype(BF16), wo_ref[...])


def merge_layer(x2, mod, ya, o_groups, lse_groups, bg, yc, z2, proj_a, proj_b, proj_c, w_o,
                *, batch, tm):
    T, D = x2.shape
    tiles_per_batch = T // batch // tm
    gate = mod[:, None, 2 * D:3 * D]
    rows = lambda width, col=0: pl.BlockSpec((tm, width), lambda i: (i, col))
    full = lambda a: pl.BlockSpec(a.shape, lambda i: (0, 0))

    def residue_major(g, width):
        dil = DIL_PATTERNS[g][1]
        assert tm % (dil * BF16_ROWS) == 0
        return pl.BlockSpec((None, dil, tm // dil, width),
                            lambda i: (i // tiles_per_batch, 0, i % tiles_per_batch, 0))

    weights = [w.astype(BF16) for w in (proj_a, proj_b, proj_c, w_o)]
    return pl.pallas_call(
        _merge_kernel,
        grid=(T // tm,),
        in_specs=[rows(D), pl.BlockSpec((None, 1, D), lambda i: (i // tiles_per_batch, 0, 0)),
                  rows(D)] + [residue_major(g, COL) for g in range(N_GROUPS)]
                 + [residue_major(g, LANES) for g in range(N_GROUPS)]
                 + [rows(COL), rows(D), rows(D, Z_MG), rows(D, Z_MG + 1), rows(D, Z_MG + 2)]
                 + [full(w) for w in weights],
        out_specs=rows(D),
        out_shape=jax.ShapeDtypeStruct((T, D), F32),
        scratch_shapes=[pltpu.VMEM((N_GROUPS, ATT_OUT // LANES, tm, LANES), F32),
                        pltpu.VMEM((N_GROUPS, tm, LANES), F32)],
        compiler_params=_params(("arbitrary",)),
        name="merge_layer",
    )(x2, gate, ya, *o_groups, *lse_groups, bg, yc, z2, z2, z2, *weights)


def kernel(x, c, positions, ada_w, ada_b, norm_g, w_in, conv_w, conv_b, lru_wa, lru_ba, lru_wx,
           lru_bx, lru_lambda, qn_g, kn_g, gla_a1, gla_a2, gla_ab, gla_on_g, proj_a, proj_b,
           proj_c, w_o):
    B, S, D = x.shape
    T = B * S
    depth = ada_w.shape[0]
    tm_proj = min(1024, S)
    tm_merge = min(512, S)
    ts_seq = min(512, S)
    tq_att = 512
    cos_t, sin_t = rope_tables(positions)
    mods = adaln_mod(c, ada_w, ada_b)
    x2 = x.reshape(T, D)
    for l in range(depth):
        z2, la, hb = proj_main(x2, mods[l], norm_g[l], w_in[l], gla_a1[l], gla_a2[l], gla_ab[l],
                               batch=B, tm=tm_proj)
        *qkv, bg = proj_att(hb, w_in[l], cos_t, sin_t, qn_g[l], kn_g[l], batch=B, tm=tm_proj)
        z3 = z2.reshape(B, S, -1)
        ya = lru_branch(z3, conv_w[l], conv_b[l], lru_wa[l], lru_ba[l], lru_wx[l], lru_bx[l],
                        lru_lambda[l], ts=ts_seq)
        outs = [attn_group(qkv[g], g, win, tq=tq_att) for g, (win, _) in enumerate(DIL_PATTERNS)]
        yc = gla_branch(z3, la.reshape(B, S, -1), gla_on_g[l], ts=ts_seq)
        x2 = merge_layer(x2, mods[l], ya.reshape(T, -1), [o for o, _ in outs],
                         [s for _, s in outs], bg, yc.reshape(T, -1), z2, proj_a[l], proj_b[l],
                         proj_c[l], w_o[l], batch=B, tm=tm_merge)
    return x2.reshape(B, S, D)
```

```python
import functools

import jax
import jax.numpy as jnp
import numpy as np
from jax import lax
from jax.experimental import pallas as pl
from jax.experimental.pallas import tpu as pltpu

LRU_BLOCKS = 16
CONV_W = 4
LRU_C = 8.0
DIL_PATTERNS = ((128, 1), (512, 4), (2048, 16))
N_GROUPS = 3
HEADS_PER_GROUP = 8
HEAD_DIM = 64
HALF = HEAD_DIM // 2
ATT_OUT = HEADS_PER_GROUP * HEAD_DIM
BAND_BLK = 128
ROPE_THETA = 10000.0
GLA_HEADS = 4
GLA_RANK = 16
GLA_NORMALIZER = 16.0
GLA_CHUNK = 64
EPS = 1e-6

LANES = 128
SUBLANES = 8
BF16_ROWS = 16
MXU_DIM = 256
VMEM_LIMIT = 48 * 1024 * 1024
VMEM_LIMIT_MAIN = 56 * 1024 * 1024

COL = 512
N_ATT_TILES = 3 * N_GROUPS
NEG_BIG = -1e30
LOG2E = float(np.log2(np.e))
LN2 = float(np.log(2.0))
BF16 = jnp.bfloat16
F32 = jnp.float32


def _params(sem, vmem_limit=VMEM_LIMIT):
    return pltpu.CompilerParams(dimension_semantics=sem, vmem_limit_bytes=vmem_limit)


def _split_bf16(x):
    hi = x.astype(BF16)
    lo = (x - hi.astype(F32)).astype(BF16)
    return hi, lo


def _dot(a, b):
    return jnp.dot(a, b, preferred_element_type=F32)


def _dot_nt(a, b):
    return lax.dot_general(a, b, (((1,), (1,)), ((), ())), preferred_element_type=F32)


def _dot_tn(a, b):
    return lax.dot_general(a, b, (((0,), (0,)), ((), ())), preferred_element_type=F32)


def _sigmoid(x):
    return 0.5 * jnp.tanh(0.5 * x) + 0.5


def _silu(x):
    h = 0.5 * x
    return h + h * jnp.tanh(h)


def _softplus(x):
    return jnp.maximum(x, 0.0) + jnp.log1p(jnp.exp(-jnp.abs(x)))


def _pair_permute_cols(w):
    shp = w.shape
    w5 = w.reshape(shp[:-1] + (shp[-1] // LANES, 2, 2, HALF))
    return jnp.swapaxes(w5, -3, -2).reshape(shp)


def _rope_kernel(pos_ref, inv_ref, sgn_ref, cos_ref, sin_ref):
    ang = pos_ref[...].astype(F32) * inv_ref[...]
    cos_ref[...] = jnp.cos(ang)
    sin_ref[...] = jnp.sin(ang) * sgn_ref[...]


def rope_tables(positions):
    T = positions.size
    inv = ROPE_THETA ** (-(np.arange(HALF, dtype=np.float32) / HALF))
    lane = np.arange(LANES)
    inv_row = jnp.asarray(inv[lane % HALF].reshape(1, LANES), F32)
    sgn_row = jnp.asarray(np.where(lane < HEAD_DIM, -1.0, 1.0).reshape(1, LANES), F32)
    rows = min(T, 2048)
    return pl.pallas_call(
        _rope_kernel,
        grid=(T // rows,),
        in_specs=[pl.BlockSpec((rows, 1), lambda i: (i, 0)),
                  pl.BlockSpec((1, LANES), lambda i: (0, 0)),
                  pl.BlockSpec((1, LANES), lambda i: (0, 0))],
        out_specs=[pl.BlockSpec((rows, LANES), lambda i: (i, 0)),
                   pl.BlockSpec((rows, LANES), lambda i: (i, 0))],
        out_shape=[jax.ShapeDtypeStruct((T, LANES), F32)] * 2,
        compiler_params=_params(("arbitrary",)),
        name="rope_tables",
    )(positions.reshape(T, 1), inv_row, sgn_row)


def _mod_kernel(c_ref, w_ref, b_ref, o_ref):
    ca = _silu(c_ref[...])
    c_hi, c_lo = _split_bf16(ca)
    w_hi, w_lo = _split_bf16(w_ref[...])
    acc = _dot(c_hi, w_hi) + _dot(c_lo, w_hi) + _dot(c_hi, w_lo)
    o_ref[...] = acc + b_ref[...]


def adaln_mod(c, ada_w, ada_b):
    depth, D, D3 = ada_w.shape
    B = c.shape[0]
    cp = jnp.pad(c, ((0, SUBLANES - B), (0, 0)))
    tn = 1024
    return pl.pallas_call(
        _mod_kernel,
        grid=(depth, D3 // tn),
        in_specs=[pl.BlockSpec((SUBLANES, D), lambda l, j: (0, 0)),
                  pl.BlockSpec((None, D, tn), lambda l, j: (l, 0, j)),
                  pl.BlockSpec((None, 1, tn), lambda l, j: (l, 0, j))],
        out_specs=pl.BlockSpec((None, SUBLANES, tn), lambda l, j: (l, 0, j)),
        out_shape=jax.ShapeDtypeStruct((depth, SUBLANES, D3), F32),
        compiler_params=_params(("arbitrary", "arbitrary")),
        name="adaln_mod",
    )(cp, ada_w, ada_b.reshape(depth, 1, D3))


def _w_in_segments(D):
    att = N_GROUPS * ATT_OUT
    seg = {'a_x': (0, D), 'a_g': (D, D)}
    for g in range(N_GROUPS):
        seg[f'q{g}'] = (2 * D + g * ATT_OUT, ATT_OUT)
        seg[f'k{g}'] = (2 * D + att + g * ATT_OUT, ATT_OUT)
        seg[f'v{g}'] = (2 * D + 2 * att + g * ATT_OUT, ATT_OUT)
    off = 2 * D + 3 * att
    for name, width in (('b_g', ATT_OUT), ('c_q', COL), ('c_k', COL), ('c_v', D), ('c_g', D),
                        ('m_g', 3 * D)):
        seg[name] = (off, width)
        off += width
    return seg, off


Z_AX, Z_AG, Z_CV, Z_CG, Z_MG, Z_CQK = 0, 1, 2, 3, 4, 7
Z_TILES = 8


def _proj_main_kernel(x0_ref, xn_ref, ng_ref, scale0_ref, shift0_ref, scalen_ref, shiftn_ref,
                      w_ref, a1_ref, a2_ref, ab_ref, cw_ref, cb_ref,
                      z_ref, la_ref, hb_ref, h_scr, hn_scr, xe_scr, *, tiles_per_batch):
    i = pl.program_id(0)
    tm, D = hb_ref.shape

    def normed(x_ref, scale_ref, shift_ref):
        xf = x_ref[...]
        ms = jnp.mean(xf * xf, axis=-1, keepdims=True)
        y = xf * lax.rsqrt(ms + EPS) * ng_ref[...]
        return (y * (1.0 + scale_ref[...]) + shift_ref[...]).astype(BF16)

    @pl.when(i == 0)
    def _():
        h_scr[...] = normed(x0_ref, scale0_ref, shift0_ref)

    @pl.when(i > 0)
    def _():
        h_scr[...] = hn_scr[...]

    first_in_batch = i % tiles_per_batch == 0

    @pl.when(first_in_batch)
    def _():
        xe_scr[0:SUBLANES, :] = jnp.zeros((SUBLANES, D), F32)

    @pl.when(jnp.logical_not(first_in_batch))
    def _():
        xe_scr[0:SUBLANES, :] = xe_scr[tm:tm + SUBLANES, :]

    hb = h_scr[...]
    hb_ref[...] = hb
    t = _dot(hb, a1_ref[...]).astype(BF16)
    pre = _dot(t, a2_ref[...]) + ab_ref[...]
    la_ref[...] = -_softplus(-pre) * (1.0 / GLA_NORMALIZER)

    def tile(c):
        return _dot(hb, w_ref[:, c * D:(c + 1) * D])

    def put(c, val):
        z_ref[:, c * D:(c + 1) * D] = val.astype(z_ref.dtype)

    ax = tile(Z_AX)
    xe_scr[SUBLANES:SUBLANES + tm, :] = ax
    xc = cb_ref[...] + ax * cw_ref[CONV_W - 1:CONV_W, :]
    for k in range(CONV_W - 1):
        off = SUBLANES - (CONV_W - 1) + k
        xc = xc + xe_scr[off:off + tm, :] * cw_ref[k:k + 1, :]
    put(Z_AX, xc)
    put(Z_AG, _silu(tile(Z_AG)))
    put(Z_CV, tile(Z_CV))
    put(Z_CG, _silu(tile(Z_CG)))
    for c in range(3):
        put(Z_MG + c, _sigmoid(tile(Z_MG + c)))
    put(Z_CQK, tile(Z_CQK))
    hn_scr[...] = normed(xn_ref, scalen_ref, shiftn_ref)


def proj_main(x2, mods, norm_g, w_main, a1p, a2p, gla_ab, conv_w, conv_b, *, layer, batch, tm):
    T, D = x2.shape
    kw = a2p.shape[-1]
    ntile = T // tm
    tiles_per_batch = ntile // batch
    nxt = lambda i: jnp.minimum(i + 1, ntile - 1)
    const = lambda shape: pl.BlockSpec(shape, lambda i: (0,) * len(shape),
                                       pipeline_mode=pl.Buffered(1))
    layer_const = lambda shape: pl.BlockSpec((None,) + shape, lambda i: (layer,) + (0,) * len(shape),
                                             pipeline_mode=pl.Buffered(1))
    mod_spec = lambda which, row: pl.BlockSpec((None, None, None, 1, D),
                                               lambda i: (layer, row(i), which, 0, 0))
    return pl.pallas_call(
        functools.partial(_proj_main_kernel, tiles_per_batch=tiles_per_batch),
        grid=(ntile,),
        in_specs=[pl.BlockSpec((tm, D), lambda i: (0, 0)),
                  pl.BlockSpec((tm, D), lambda i: (nxt(i), 0)),
                  layer_const((1, D)),
                  mod_spec(1, lambda i: 0), mod_spec(0, lambda i: 0),
                  mod_spec(1, lambda i: nxt(i) // tiles_per_batch),
                  mod_spec(0, lambda i: nxt(i) // tiles_per_batch),
                  layer_const((D, Z_TILES * D)),
                  layer_const((D, LANES)), layer_const((LANES, kw)), layer_const((1, kw)),
                  layer_const((CONV_W, D)), layer_const((1, D))],
        out_specs=[pl.BlockSpec((tm, Z_TILES * D), lambda i: (i, 0)),
                   pl.BlockSpec((tm, kw), lambda i: (i, 0)),
                   pl.BlockSpec((tm, D), lambda i: (i, 0))],
        out_shape=[jax.ShapeDtypeStruct((T, Z_TILES * D), BF16),
                   jax.ShapeDtypeStruct((T, kw), F32),
                   jax.ShapeDtypeStruct((T, D), BF16)],
        scratch_shapes=[pltpu.VMEM((tm, D), BF16), pltpu.VMEM((tm, D), BF16),
                        pltpu.VMEM((tm + SUBLANES, D), F32)],
        compiler_params=_params(("arbitrary",), VMEM_LIMIT_MAIN),
        name="proj_main",
    )(x2, x2, norm_g, mods, mods, mods, mods, w_main, a1p, a2p, gla_ab, conv_w, conv_b)


def _proj_att_kernel(h_ref, w_ref, cos_ref, sin_ref, gain_ref, ones_ref,
                     qkv0_ref, qkv1_ref, qkv2_ref, bg_ref, val_scr):
    tm = h_ref.shape[0]
    nchunk = COL // LANES
    h = h_ref[...]
    ones = ones_ref[...]
    cos = cos_ref[...]
    sin = sin_ref[...]
    bg_ref[...] = _silu(_dot(h, w_ref[:, N_ATT_TILES * COL:])).astype(bg_ref.dtype)
    for g, out_ref in enumerate((qkv0_ref, qkv1_ref, qkv2_ref)):
        dil = DIL_PATTERNS[g][1]
        for kind in range(3):
            t = 3 * g + kind
            acc = _dot(h, w_ref[:, t * COL:(t + 1) * COL])
            if kind == 2:
                chunks = [acc[:, c * LANES:(c + 1) * LANES] for c in range(nchunk)]
            else:
                qscale = HEAD_DIM ** -0.5 * LOG2E if kind == 0 else 1.0
                chunks = []
                for c2 in range(COL // MXU_DIM):
                    sl2 = slice(c2 * MXU_DIM, (c2 + 1) * MXU_DIM)
                    a2 = acc[:, sl2]
                    ssq = _dot((a2 * a2).astype(BF16), ones)
                    xn = a2 * lax.rsqrt(ssq * (1.0 / HEAD_DIM) + EPS) * (gain_ref[t] * qscale)[:, sl2]
                    for cc in range(MXU_DIM // LANES):
                        xc = xn[:, cc * LANES:(cc + 1) * LANES]
                        chunks.append(xc * cos + pltpu.roll(xc, HEAD_DIM, 1) * sin)
            for c, val in enumerate(chunks):
                sl = slice(c * LANES, (c + 1) * LANES)
                if dil == 1:
                    out_ref[kind, 0, :, sl] = val.astype(out_ref.dtype)
                else:
                    slot = (t * nchunk + c) % val_scr.shape[0]
                    val_scr[slot] = val
                    for r in range(dil):
                        out_ref[kind, r, :, sl] = (
                            val_scr[slot, pl.ds(r, tm // dil, stride=dil), :].astype(out_ref.dtype))


def split_w_in(w_in, qn_g, kn_g):
    depth, D, _ = w_in.shape
    seg, total = _w_in_segments(D)
    assert total == w_in.shape[2] and 2 * COL == D
    cut = lambda name: w_in[:, :, seg[name][0]:seg[name][0] + seg[name][1]]
    w_main = jnp.concatenate([cut(n) for n in ('a_x', 'a_g', 'c_v', 'c_g', 'm_g', 'c_q', 'c_k')],
                             axis=2).astype(BF16)
    cols = []
    for g in range(N_GROUPS):
        cols += [_pair_permute_cols(cut(f'q{g}')), _pair_permute_cols(cut(f'k{g}')), cut(f'v{g}')]
    cols.append(cut('b_g'))
    w_att = jnp.concatenate(cols, axis=2).astype(BF16)
    tiled = lambda g: _pair_permute_cols(jnp.tile(g, (1, 1, HEADS_PER_GROUP)))
    gains = jnp.stack([tiled(qn_g), tiled(kn_g), jnp.ones((depth, N_GROUPS, COL), F32)], axis=2)
    return w_main, w_att, gains.reshape(depth, N_ATT_TILES, 1, COL)


def proj_att(hb, w_att, cos_t, sin_t, gains, *, layer, batch, tm):
    T, D = hb.shape
    S = T // batch
    tiles_per_batch = S // tm
    ntile = N_ATT_TILES + 1
    lane = np.arange(MXU_DIM)
    head = (lane // LANES) * 2 + (lane % HEAD_DIM) // HALF
    ones_bd = jnp.asarray(head[:, None] == head[None, :], BF16)

    def qkv_spec(g):
        dil = DIL_PATTERNS[g][1]
        assert tm % (dil * BF16_ROWS) == 0
        return pl.BlockSpec((None, 3, dil, tm // dil, COL),
                            lambda i: (i // tiles_per_batch, 0, 0, i % tiles_per_batch, 0))

    def qkv_shape(g):
        dil = DIL_PATTERNS[g][1]
        return jax.ShapeDtypeStruct((batch, 3, dil, S // dil, COL), BF16)

    const = lambda shape: pl.BlockSpec(shape, lambda i: (0,) * len(shape),
                                       pipeline_mode=pl.Buffered(1))
    layer_const = lambda shape: pl.BlockSpec((None,) + shape, lambda i: (layer,) + (0,) * len(shape),
                                             pipeline_mode=pl.Buffered(1))
    return pl.pallas_call(
        _proj_att_kernel,
        grid=(T // tm,),
        in_specs=[pl.BlockSpec((tm, D), lambda i: (i, 0)),
                  layer_const((D, ntile * COL)),
                  pl.BlockSpec((tm, LANES), lambda i: (i, 0)),
                  pl.BlockSpec((tm, LANES), lambda i: (i, 0)),
                  layer_const((N_ATT_TILES, 1, COL)),
                  const((MXU_DIM, MXU_DIM))],
        out_specs=[qkv_spec(0), qkv_spec(1), qkv_spec(2),
                   pl.BlockSpec((tm, COL), lambda i: (i, 0))],
        out_shape=[qkv_shape(0), qkv_shape(1), qkv_shape(2),
                   jax.ShapeDtypeStruct((T, COL), BF16)],
        scratch_shapes=[pltpu.VMEM((2 * (COL // LANES), tm, LANES), F32)],
        compiler_params=_params(("arbitrary",)),
        name="proj_att",
    )(hb, w_att, cos_t, sin_t, gains, ones_bd)


def _lru_tile(xc_ref, wa_ref, ba_ref, wx_ref, bx_ref, lam_ref, a_scr, u_scr, hc_scr):
    ts, W = xc_ref.shape
    neg_c_sp = -LRU_C * _softplus(-lam_ref[...])
    for g in range(W // MXU_DIM):
        sl = slice(g * MXU_DIM, (g + 1) * MXU_DIM)
        xb = xc_ref[:, sl]
        xg = xb.astype(F32)
        r = _sigmoid(_dot(xb, wa_ref[g]) + ba_ref[:, sl])
        i = _sigmoid(_dot(xb, wx_ref[g]) + bx_ref[:, sl])
        log_a = neg_c_sp[:, sl] * r
        a = jnp.exp(log_a)
        a_scr[:, sl] = a
        y = jnp.tanh(-log_a) * (1.0 + a * a)
        mult = jnp.where(y > 0.0, y * lax.rsqrt(y), 0.0)
        u_scr[:, sl] = mult * (i * xg)

    row = lax.broadcasted_iota(jnp.int32, (SUBLANES, W), 0)
    h_prev = hc_scr[...]
    for t in range(ts // SUBLANES):
        rs = slice(t * SUBLANES, (t + 1) * SUBLANES)
        a = a_scr[rs, :]
        u = u_scr[rs, :]
        d = 1
        while d < SUBLANES:
            keep = row >= d
            a_sh = jnp.where(keep, pltpu.roll(a, d, 0), 1.0)
            u_sh = jnp.where(keep, pltpu.roll(u, d, 0), 0.0)
            u = a * u_sh + u
            a = a * a_sh
            d *= 2
        h = a * h_prev + u
        u_scr[rs, :] = h
        h_prev = jnp.broadcast_to(h[SUBLANES - 1:SUBLANES, :], (SUBLANES, W))
    hc_scr[...] = h_prev


def lru_block_diag(w):
    depth, nblk, bw, _ = w.shape
    per = MXU_DIM // bw
    w5 = w.reshape(depth, nblk // per, per, bw, bw)
    eye = jnp.eye(per, dtype=w.dtype)
    return jnp.einsum('lgpjk,pq->lgpjqk', w5, eye).reshape(depth, nblk // per, MXU_DIM, MXU_DIM).astype(BF16)


def _attn_kernel(q_ref, kp_ref, kc_ref, vp_ref, vc_ref, o_ref, lse_ref, s_scr, p_scr, *, span, tq):
    n = pl.program_id(2)
    blk = BAND_BLK
    nqb = tq // blk
    npair = COL // LANES
    qi = lax.broadcasted_iota(jnp.int32, (blk, 2 * blk), 0)
    kj = lax.broadcasted_iota(jnp.int32, (blk, 2 * blk), 1)
    rel = qi + blk - kj
    band = (rel >= 0) & (rel <= span)
    bias = jnp.where(band, 0.0, NEG_BIG).astype(F32)
    bias_first = jnp.where(band & ((kj >= blk) | (n > 0)), 0.0, NEG_BIG).astype(F32)
    k = jnp.concatenate([kp_ref[...], kc_ref[...]], axis=0)
    v = jnp.concatenate([vp_ref[...], vc_ref[...]], axis=0)
    lane_k = lax.broadcasted_iota(jnp.int32, (blk + tq, LANES), 1)
    lane_q = lax.broadcasted_iota(jnp.int32, (tq, LANES), 1)
    lane = lax.broadcasted_iota(jnp.int32, (blk, LANES), 1)
    lanes_per_head = LANES // HEADS_PER_GROUP
    zero = jnp.zeros((), BF16)
    tile = lambda c, qb, hh: (c * nqb + qb) * 2 + hh

    for c in range(npair):
        sl = slice(c * LANES, (c + 1) * LANES)
        q_c = q_ref[:, sl]
        k_c = k[:, sl]
        q_heads = (jnp.where((lane_q % HEAD_DIM) < HALF, q_c, zero),
                   jnp.where((lane_q % HEAD_DIM) >= HALF, q_c, zero))
        for qb in range(nqb):
            for hh in range(2):
                s = _dot_nt(q_heads[hh][qb * blk:(qb + 1) * blk], k_c[qb * blk:(qb + 2) * blk])
                s_scr[tile(c, qb, hh)] = s + (bias_first if qb == 0 else bias)

    inv_l = {}
    lse_tiles = [jnp.zeros((blk, LANES), F32) for _ in range(nqb)]
    for c in range(npair):
        for qb in range(nqb):
            for hh in range(2):
                t = tile(c, qb, hh)
                s = s_scr[t]
                m = jnp.max(s, axis=-1, keepdims=True)
                p = jnp.exp2(s - m)
                l = jnp.sum(p, axis=-1, keepdims=True)
                p_scr[t] = p.astype(BF16)
                inv_l[t] = 1.0 / l
                h = 2 * c + hh
                lse = m * LN2 + jnp.log(l)
                lse_tiles[qb] = jnp.where(lane // lanes_per_head == h, lse, lse_tiles[qb])

    for c in range(npair):
        sl = slice(c * LANES, (c + 1) * LANES)
        v_c = v[:, sl]
        v_heads = (jnp.where(lane_k < HEAD_DIM, v_c, zero), jnp.where(lane_k >= HEAD_DIM, v_c, zero))
        for qb in range(nqb):
            ks = slice(qb * blk, (qb + 2) * blk)
            o_pair = (_dot(p_scr[tile(c, qb, 0)], v_heads[0][ks])
                      + _dot(p_scr[tile(c, qb, 1)], v_heads[1][ks]))
            scale = jnp.where(lane < HEAD_DIM, inv_l[tile(c, qb, 0)], inv_l[tile(c, qb, 1)])
            o_ref[qb * blk:(qb + 1) * blk, sl] = (o_pair * scale).astype(o_ref.dtype)
    for qb in range(nqb):
        lse_ref[qb * blk:(qb + 1) * blk, :] = lse_tiles[qb]


def attn_group(qkv, g, window, *, tq):
    B, _, dil, L, _ = qkv.shape
    tq = min(tq, L)
    assert L % tq == 0 and tq % BAND_BLK == 0
    per = tq // BAND_BLK
    cur = lambda kind: pl.BlockSpec((None, None, None, tq, COL), lambda b, r, n: (b, kind, r, n, 0))
    prev = lambda kind: pl.BlockSpec((None, None, None, BAND_BLK, COL),
                                     lambda b, r, n: (b, kind, r, jnp.maximum(n * per - 1, 0), 0))
    return pl.pallas_call(
        functools.partial(_attn_kernel, span=window // dil, tq=tq),
        grid=(B, dil, L // tq),
        in_specs=[cur(0), prev(1), cur(1), prev(2), cur(2)],
        out_specs=[pl.BlockSpec((None, None, tq, COL), lambda b, r, n: (b, r, n, 0)),
                   pl.BlockSpec((None, None, tq, LANES), lambda b, r, n: (b, r, n, 0))],
        out_shape=[jax.ShapeDtypeStruct((B, dil, L, COL), BF16),
                   jax.ShapeDtypeStruct((B, dil, L, LANES), F32)],
        scratch_shapes=[pltpu.VMEM((HEADS_PER_GROUP * per, BAND_BLK, 2 * BAND_BLK), F32),
                        pltpu.VMEM((HEADS_PER_GROUP * per, BAND_BLK, 2 * BAND_BLK), BF16)],
        compiler_params=_params(("arbitrary", "arbitrary", "arbitrary")),
        name=f"attn_group{g}",
    )(qkv, qkv, qkv, qkv, qkv)


def _gla_kernel(q_ref, k_ref, v_ref, g_ref, la_ref, og_ref, y_ref, st_scr, *, ts):
    s = pl.program_id(1)
    C = GLA_CHUNK
    kw = q_ref.shape[-1]
    dk = kw // GLA_HEADS
    dv = v_ref.shape[-1] // GLA_HEADS

    @pl.when(s == 0)
    def _():
        st_scr[...] = jnp.zeros_like(st_scr)

    row = lax.broadcasted_iota(jnp.int32, (C, kw), 0)
    qi = lax.broadcasted_iota(jnp.int32, (C, C), 0)
    kj = lax.broadcasted_iota(jnp.int32, (C, C), 1)
    causal = qi >= kj
    for c in range(ts // C):
        rs = slice(c * C, (c + 1) * C)
        b = la_ref[rs, :]
        d = 1
        while d < C:
            b = b + jnp.where(row >= d, pltpu.roll(b, d, 0), 0.0)
            d *= 2
        b_last = b[C - 1:C, :]
        qf = q_ref[rs, :].astype(F32)
        kf = k_ref[rs, :].astype(F32)
        q_dec = (qf * (dk ** -0.5) * jnp.exp(b)).astype(BF16)
        k_inv = (kf * jnp.exp(-b)).astype(BF16)
        k_end = (kf * jnp.exp(b_last - b)).astype(BF16)
        dec = jnp.exp(b_last)
        for h in range(GLA_HEADS):
            ks = slice(h * dk, (h + 1) * dk)
            vs = slice(h * dv, (h + 1) * dv)
            vh = v_ref[rs, vs]
            att = jnp.where(causal, _dot_nt(q_dec[:, ks], k_inv[:, ks]), 0.0)
            st = st_scr[h]
            o = _dot(att.astype(BF16), vh) + _dot_nt(q_dec[:, ks], st.astype(BF16))
            st_scr[h] = st * dec[:, ks] + _dot_tn(vh, k_end[:, ks])
            ms = jnp.mean(o * o, axis=-1, keepdims=True)
            yn = o * lax.rsqrt(ms + EPS) * og_ref[...]
            y_ref[rs, vs] = (yn * g_ref[rs, vs].astype(F32)).astype(y_ref.dtype)


def gla_branch(z3, la3, gla_on_g, *, layer, ts):
    B, S, _ = z3.shape
    kw = la3.shape[-1]
    dv = gla_on_g.shape[-1]
    vw = GLA_HEADS * dv
    dk = kw // GLA_HEADS
    assert vw == 2 * kw
    zmap = lambda width, col: pl.BlockSpec((None, ts, width), lambda b, s: (b, s, col))
    return pl.pallas_call(
        functools.partial(_gla_kernel, ts=ts),
        grid=(B, S // ts),
        in_specs=[zmap(kw, 2 * Z_CQK), zmap(kw, 2 * Z_CQK + 1), zmap(vw, Z_CV), zmap(vw, Z_CG),
                  pl.BlockSpec((None, ts, kw), lambda b, s: (b, s, 0)),
                  pl.BlockSpec((None, 1, dv), lambda b, s: (layer, 0, 0))],
        out_specs=pl.BlockSpec((None, ts, vw), lambda b, s: (b, s, 0)),
        out_shape=jax.ShapeDtypeStruct((B, S, vw), BF16),
        scratch_shapes=[pltpu.VMEM((GLA_HEADS, dv, dk), F32)],
        compiler_params=_params(("arbitrary", "arbitrary")),
        name="gla_branch",
    )(z3, z3, z3, z3, la3, gla_on_g)


def _merge_kernel(x_ref, gate_ref, xc_ref, ag_ref, o0_ref, o1_ref, o2_ref, l0_ref, l1_ref, l2_ref,
                  bg_ref, yc_ref, m0_ref, m1_ref, m2_ref, wa_ref, ba_ref, wx_ref, bx_ref, lam_ref,
                  pa_ref, pb_ref, pc_ref, wo_ref, out_ref,
                  o_scr, l_scr, a_scr, u_scr, hc_scr, *, tiles_per_batch):
    tm = x_ref.shape[0]

    @pl.when(pl.program_id(0) % tiles_per_batch == 0)
    def _():
        hc_scr[...] = jnp.zeros_like(hc_scr)

    _lru_tile(xc_ref, wa_ref, ba_ref, wx_ref, bx_ref, lam_ref, a_scr, u_scr, hc_scr)

    for g, (o_ref, l_ref) in enumerate(((o0_ref, l0_ref), (o1_ref, l1_ref), (o2_ref, l2_ref))):
        dil = DIL_PATTERNS[g][1]
        for r in range(dil):
            rows = pl.ds(r, tm // dil, stride=dil) if dil > 1 else slice(None)
            l_scr[g, rows, :] = l_ref[r]
            for c in range(ATT_OUT // LANES):
                o_scr[g, c, rows, :] = o_ref[r, :, c * LANES:(c + 1) * LANES].astype(F32)
    l0, l1, l2 = l_scr[0], l_scr[1], l_scr[2]
    mx = jnp.maximum(jnp.maximum(l0, l1), l2)
    e0, e1, e2 = jnp.exp(l0 - mx), jnp.exp(l1 - mx), jnp.exp(l2 - mx)
    inv_den = 1.0 / (e0 + e1 + e2)
    wts = (e0 * inv_den, e1 * inv_den, e2 * inv_den)
    lanes_per_head = LANES // HEADS_PER_GROUP
    lane = lax.broadcasted_iota(jnp.int32, (tm, LANES), 1)
    upper = lane >= HEAD_DIM
    cols = []
    for c in range(ATT_OUT // LANES):
        acc = jnp.zeros((tm, LANES), F32)
        for g, w in enumerate(wts):
            h0 = 2 * c * lanes_per_head
            h1 = (2 * c + 1) * lanes_per_head
            wv = jnp.where(upper, w[:, h1:h1 + 1], w[:, h0:h0 + 1])
            acc = acc + wv * o_scr[g, c]
        cols.append(acc)
    yb = (jnp.concatenate(cols, axis=-1) * bg_ref[...].astype(F32)).astype(BF16)
    half = tm // 2
    for r in range(2):
        rs = slice(r * half, (r + 1) * half)
        ya = (u_scr[rs, :] * ag_ref[rs, :].astype(F32)).astype(BF16)
        merged = (m0_ref[rs, :].astype(F32) * _dot(ya, pa_ref[...])
                  + m1_ref[rs, :].astype(F32) * _dot(yb[rs], pb_ref[...])
                  + m2_ref[rs, :].astype(F32) * _dot(yc_ref[rs, :], pc_ref[...]))
        out_ref[rs, :] = x_ref[rs, :] + gate_ref[...] * _dot(merged.astype(BF16), wo_ref[...])


def merge_layer(x2, mods, o_groups, lse_groups, bg, yc, z2, lru_params, proj_a, proj_b, proj_c,
                w_o, *, layer, batch, tm):
    T, D = x2.shape
    tiles_per_batch = T // batch // tm
    rows = lambda width, col=0: pl.BlockSpec((tm, width), lambda i: (i, col))
    full = lambda a: pl.BlockSpec((None,) + a.shape[1:], lambda i: (layer,) + (0,) * (a.ndim - 1),
                                  pipeline_mode=pl.Buffered(1))

    def residue_major(g, width):
        dil = DIL_PATTERNS[g][1]
        assert tm % (dil * BF16_ROWS) == 0
        return pl.BlockSpec((None, dil, tm // dil, width),
                            lambda i: (i // tiles_per_batch, 0, i % tiles_per_batch, 0))

    weights = (proj_a, proj_b, proj_c, w_o)
    gate_spec = pl.BlockSpec((None, None, None, 1, D), lambda i: (layer, i // tiles_per_batch, 2, 0, 0))
    return pl.pallas_call(
        functools.partial(_merge_kernel, tiles_per_batch=tiles_per_batch),
        grid=(T // tm,),
        in_specs=[rows(D), gate_spec, rows(D, Z_AX), rows(D, Z_AG)]
                 + [residue_major(g, COL) for g in range(N_GROUPS)]
                 + [residue_major(g, LANES) for g in range(N_GROUPS)]
                 + [rows(COL), rows(D), rows(D, Z_MG), rows(D, Z_MG + 1), rows(D, Z_MG + 2)]
                 + [full(p) for p in lru_params] + [full(w) for w in weights],
        out_specs=rows(D),
        out_shape=jax.ShapeDtypeStruct((T, D), F32),
        scratch_shapes=[pltpu.VMEM((N_GROUPS, ATT_OUT // LANES, tm, LANES), F32),
                        pltpu.VMEM((N_GROUPS, tm, LANES), F32),
                        pltpu.VMEM((tm, D), F32), pltpu.VMEM((tm, D), F32),
                        pltpu.VMEM((SUBLANES, D), F32)],
        compiler_params=_params(("arbitrary",)),
        name="merge_layer",
    )(x2, mods, z2, z2, *o_groups, *lse_groups, bg, yc, z2, z2, z2, *lru_params, *weights)


def kernel(x, c, positions, ada_w, ada_b, norm_g, w_in, conv_w, conv_b, lru_wa, lru_ba, lru_wx,
           lru_bx, lru_lambda, qn_g, kn_g, gla_a1, gla_a2, gla_ab, gla_on_g, proj_a, proj_b,
           proj_c, w_o):
    B, S, D = x.shape
    T = B * S
    depth = ada_w.shape[0]
    tm_main = min(512, S)
    tm_att = min(1024, S)
    tm_merge = min(512, S)
    ts_seq = min(512, S)
    tq_att = 512
    row = lambda v: v[:, None, :]
    w_main, w_att, gains = split_w_in(w_in, qn_g, kn_g)
    a1p = jnp.pad(gla_a1, ((0, 0), (0, 0), (0, LANES - GLA_RANK))).astype(BF16)
    a2p = jnp.pad(gla_a2, ((0, 0), (0, LANES - GLA_RANK), (0, 0))).astype(BF16)
    lru_params = (lru_block_diag(lru_wa), row(lru_ba), lru_block_diag(lru_wx), row(lru_bx),
                  row(lru_lambda))
    out_w = [w.astype(BF16) for w in (proj_a, proj_b, proj_c, w_o)]

    cos_t, sin_t = rope_tables(positions)
    mods = adaln_mod(c, ada_w, ada_b).reshape(depth, SUBLANES, 3, 1, D)
    x2 = x.reshape(T, D)
    for l in range(depth):
        z2, la, hb = proj_main(x2, mods, row(norm_g), w_main, a1p, a2p, row(gla_ab), conv_w,
                               row(conv_b), layer=l, batch=B, tm=tm_main)
        *qkv, bg = proj_att(hb, w_att, cos_t, sin_t, gains, layer=l, batch=B, tm=tm_att)
        z3 = z2.reshape(B, S, -1)
        outs = [attn_group(qkv[g], g, win, tq=tq_att) for g, (win, _) in enumerate(DIL_PATTERNS)]
        yc = gla_branch(z3, la.reshape(B, S, -1), row(gla_on_g), layer=l, ts=ts_seq)
        x2 = merge_layer(x2, mods, [o for o, _ in outs], [s for _, s in outs], bg,
                         yc.reshape(T, -1), z2, lru_params, *out_w, layer=l, batch=B, tm=tm_merge)
    return x2.reshape(B, S, D)
```

```python
import functools

import jax
import jax.numpy as jnp
import numpy as np
from jax import lax
from jax.experimental import pallas as pl
from jax.experimental.pallas import tpu as pltpu

LRU_BLOCKS = 16
CONV_W = 4
LRU_C = 8.0
DIL_PATTERNS = ((128, 1), (512, 4), (2048, 16))
N_GROUPS = 3
HEADS_PER_GROUP = 8
HEAD_DIM = 64
HALF = HEAD_DIM // 2
ATT_OUT = HEADS_PER_GROUP * HEAD_DIM
BAND_BLK = 128
ROPE_THETA = 10000.0
GLA_HEADS = 4
GLA_RANK = 16
GLA_NORMALIZER = 16.0
GLA_CHUNK = 64
EPS = 1e-6

LANES = 128
SUBLANES = 8
BF16_ROWS = 16
MXU_DIM = 256
VMEM_LIMIT = 48 * 1024 * 1024
VMEM_LIMIT_MAIN = 56 * 1024 * 1024

COL = 512
N_ATT_TILES = 3 * N_GROUPS
NEG_BIG = -1e30
LOG2E = float(np.log2(np.e))
LN2 = float(np.log(2.0))
BF16 = jnp.bfloat16
F32 = jnp.float32


def _params(sem, vmem_limit=VMEM_LIMIT):
    return pltpu.CompilerParams(dimension_semantics=sem, vmem_limit_bytes=vmem_limit)


def _split_bf16(x):
    hi = x.astype(BF16)
    lo = (x - hi.astype(F32)).astype(BF16)
    return hi, lo


def _dot(a, b):
    return jnp.dot(a, b, preferred_element_type=F32)


def _dot_nt(a, b):
    return lax.dot_general(a, b, (((1,), (1,)), ((), ())), preferred_element_type=F32)


def _dot_tn(a, b):
    return lax.dot_general(a, b, (((0,), (0,)), ((), ())), preferred_element_type=F32)


def _sigmoid(x):
    return 0.5 * jnp.tanh(0.5 * x) + 0.5


def _silu(x):
    h = 0.5 * x
    return h + h * jnp.tanh(h)


def _softplus(x):
    return jnp.maximum(x, 0.0) + jnp.log1p(jnp.exp(-jnp.abs(x)))


def _pair_permute_cols(w):
    shp = w.shape
    w5 = w.reshape(shp[:-1] + (shp[-1] // LANES, 2, 2, HALF))
    return jnp.swapaxes(w5, -3, -2).reshape(shp)


def _rope_kernel(pos_ref, inv_ref, sgn_ref, cos_ref, sin_ref):
    ang = pos_ref[...].astype(F32) * inv_ref[...]
    cos_ref[...] = jnp.cos(ang)
    sin_ref[...] = jnp.sin(ang) * sgn_ref[...]


def rope_tables(positions):
    T = positions.size
    inv = ROPE_THETA ** (-(np.arange(HALF, dtype=np.float32) / HALF))
    lane = np.arange(LANES)
    inv_row = jnp.asarray(inv[lane % HALF].reshape(1, LANES), F32)
    sgn_row = jnp.asarray(np.where(lane < HEAD_DIM, -1.0, 1.0).reshape(1, LANES), F32)
    rows = min(T, 2048)
    return pl.pallas_call(
        _rope_kernel,
        grid=(T // rows,),
        in_specs=[pl.BlockSpec((rows, 1), lambda i: (i, 0)),
                  pl.BlockSpec((1, LANES), lambda i: (0, 0)),
                  pl.BlockSpec((1, LANES), lambda i: (0, 0))],
        out_specs=[pl.BlockSpec((rows, LANES), lambda i: (i, 0)),
                   pl.BlockSpec((rows, LANES), lambda i: (i, 0))],
        out_shape=[jax.ShapeDtypeStruct((T, LANES), F32)] * 2,
        compiler_params=_params(("arbitrary",)),
        name="rope_tables",
    )(positions.reshape(T, 1), inv_row, sgn_row)


def _mod_kernel(c_ref, w_ref, b_ref, o_ref):
    ca = _silu(c_ref[...])
    c_hi, c_lo = _split_bf16(ca)
    w_hi, w_lo = _split_bf16(w_ref[...])
    acc = _dot(c_hi, w_hi) + _dot(c_lo, w_hi) + _dot(c_hi, w_lo)
    o_ref[...] = acc + b_ref[...]


def adaln_mod(c, ada_w, ada_b):
    depth, D, D3 = ada_w.shape
    B = c.shape[0]
    cp = jnp.pad(c, ((0, SUBLANES - B), (0, 0)))
    tn = 1024
    return pl.pallas_call(
        _mod_kernel,
        grid=(depth, D3 // tn),
        in_specs=[pl.BlockSpec((SUBLANES, D), lambda l, j: (0, 0)),
                  pl.BlockSpec((None, D, tn), lambda l, j: (l, 0, j)),
                  pl.BlockSpec((None, 1, tn), lambda l, j: (l, 0, j))],
        out_specs=pl.BlockSpec((None, SUBLANES, tn), lambda l, j: (l, 0, j)),
        out_shape=jax.ShapeDtypeStruct((depth, SUBLANES, D3), F32),
        compiler_params=_params(("arbitrary", "arbitrary")),
        name="adaln_mod",
    )(cp, ada_w, ada_b.reshape(depth, 1, D3))


def _w_in_segments(D):
    att = N_GROUPS * ATT_OUT
    seg = {'a_x': (0, D), 'a_g': (D, D)}
    for g in range(N_GROUPS):
        seg[f'q{g}'] = (2 * D + g * ATT_OUT, ATT_OUT)
        seg[f'k{g}'] = (2 * D + att + g * ATT_OUT, ATT_OUT)
        seg[f'v{g}'] = (2 * D + 2 * att + g * ATT_OUT, ATT_OUT)
    off = 2 * D + 3 * att
    for name, width in (('b_g', ATT_OUT), ('c_q', COL), ('c_k', COL), ('c_v', D), ('c_g', D),
                        ('m_g', 3 * D)):
        seg[name] = (off, width)
        off += width
    return seg, off


Z_AX, Z_AG, Z_CV, Z_CG, Z_MG, Z_CQK = 0, 1, 2, 3, 4, 7
Z_TILES = 8


def _proj_main_kernel(x0_ref, xn_ref, ng_ref, scale0_ref, shift0_ref, scalen_ref, shiftn_ref,
                      w_ref, a1_ref, a2_ref, ab_ref, cw_ref, cb_ref,
                      z_ref, la_ref, hb_ref, h_scr, hn_scr, xe_scr, *, tiles_per_batch):
    i = pl.program_id(0)
    tm, D = hb_ref.shape

    def normed(x_ref, scale_ref, shift_ref):
        xf = x_ref[...]
        ms = jnp.mean(xf * xf, axis=-1, keepdims=True)
        y = xf * lax.rsqrt(ms + EPS) * ng_ref[...]
        return (y * (1.0 + scale_ref[...]) + shift_ref[...]).astype(BF16)

    @pl.when(i == 0)
    def _():
        h_scr[...] = normed(x0_ref, scale0_ref, shift0_ref)

    @pl.when(i > 0)
    def _():
        h_scr[...] = hn_scr[...]

    first_in_batch = i % tiles_per_batch == 0

    @pl.when(first_in_batch)
    def _():
        xe_scr[0:SUBLANES, :] = jnp.zeros((SUBLANES, D), F32)

    @pl.when(jnp.logical_not(first_in_batch))
    def _():
        xe_scr[0:SUBLANES, :] = xe_scr[tm:tm + SUBLANES, :]

    hb = h_scr[...]
    hb_ref[...] = hb
    t = _dot(hb, a1_ref[...]).astype(BF16)
    pre = _dot(t, a2_ref[...]) + ab_ref[...]
    la_ref[...] = -_softplus(-pre) * (1.0 / GLA_NORMALIZER)

    def tile(c):
        return _dot(hb, w_ref[:, c * D:(c + 1) * D])

    def put(c, val):
        z_ref[:, c * D:(c + 1) * D] = val.astype(z_ref.dtype)

    ax = tile(Z_AX)
    xe_scr[SUBLANES:SUBLANES + tm, :] = ax
    xc = cb_ref[...] + ax * cw_ref[CONV_W - 1:CONV_W, :]
    for k in range(CONV_W - 1):
        off = SUBLANES - (CONV_W - 1) + k
        xc = xc + xe_scr[off:off + tm, :] * cw_ref[k:k + 1, :]
    put(Z_AX, xc)
    put(Z_AG, _silu(tile(Z_AG)))
    put(Z_CV, tile(Z_CV))
    put(Z_CG, _silu(tile(Z_CG)))
    for c in range(3):
        put(Z_MG + c, _sigmoid(tile(Z_MG + c)))
    put(Z_CQK, tile(Z_CQK))
    hn_scr[...] = normed(xn_ref, scalen_ref, shiftn_ref)


def proj_main(x2, mods, norm_g, w_main, a1p, a2p, gla_ab, conv_w, conv_b, *, layer, batch, tm):
    T, D = x2.shape
    kw = a2p.shape[-1]
    ntile = T // tm
    tiles_per_batch = ntile // batch
    nxt = lambda i: jnp.minimum(i + 1, ntile - 1)
    const = lambda shape: pl.BlockSpec(shape, lambda i: (0,) * len(shape),
                                       pipeline_mode=pl.Buffered(1))
    layer_const = lambda shape: pl.BlockSpec((None,) + shape, lambda i: (layer,) + (0,) * len(shape),
                                             pipeline_mode=pl.Buffered(1))
    mod_spec = lambda which, row: pl.BlockSpec((None, None, None, 1, D),
                                               lambda i: (layer, row(i), which, 0, 0))
    return pl.pallas_call(
        functools.partial(_proj_main_kernel, tiles_per_batch=tiles_per_batch),
        grid=(ntile,),
        in_specs=[pl.BlockSpec((tm, D), lambda i: (0, 0)),
                  pl.BlockSpec((tm, D), lambda i: (nxt(i), 0)),
                  layer_const((1, D)),
                  mod_spec(1, lambda i: 0), mod_spec(0, lambda i: 0),
                  mod_spec(1, lambda i: nxt(i) // tiles_per_batch),
                  mod_spec(0, lambda i: nxt(i) // tiles_per_batch),
                  layer_const((D, Z_TILES * D)),
                  layer_const((D, LANES)), layer_const((LANES, kw)), layer_const((1, kw)),
                  layer_const((CONV_W, D)), layer_const((1, D))],
        out_specs=[pl.BlockSpec((tm, Z_TILES * D), lambda i: (i, 0)),
                   pl.BlockSpec((tm, kw), lambda i: (i, 0)),
                   pl.BlockSpec((tm, D), lambda i: (i, 0))],
        out_shape=[jax.ShapeDtypeStruct((T, Z_TILES * D), BF16),
                   jax.ShapeDtypeStruct((T, kw), F32),
                   jax.ShapeDtypeStruct((T, D), BF16)],
        scratch_shapes=[pltpu.VMEM((tm, D), BF16), pltpu.VMEM((tm, D), BF16),
                        pltpu.VMEM((tm + SUBLANES, D), F32)],
        compiler_params=_params(("arbitrary",), VMEM_LIMIT_MAIN),
        name="proj_main",
    )(x2, x2, norm_g, mods, mods, mods, mods, w_main, a1p, a2p, gla_ab, conv_w, conv_b)


def _proj_att_kernel(h_ref, w_ref, cos_ref, sin_ref, gain_ref, ones_ref,
                     qkv0_ref, qkv1_ref, qkv2_ref, bg_ref, val_scr):
    tm = h_ref.shape[0]
    nchunk = COL // LANES
    h = h_ref[...]
    ones = ones_ref[...]
    cos = cos_ref[...]
    sin = sin_ref[...]
    bg_ref[...] = _silu(_dot(h, w_ref[:, N_ATT_TILES * COL:])).astype(bg_ref.dtype)
    for g, out_ref in enumerate((qkv0_ref, qkv1_ref, qkv2_ref)):
        dil = DIL_PATTERNS[g][1]
        for kind in range(3):
            t = 3 * g + kind
            acc = _dot(h, w_ref[:, t * COL:(t + 1) * COL])

            def emit(c, val, t=t, kind=kind, dil=dil, out_ref=out_ref):
                sl = slice(c * LANES, (c + 1) * LANES)
                if dil == 1:
                    out_ref[kind, 0, :, sl] = val.astype(out_ref.dtype)
                else:
                    slot = (t * nchunk + c) % val_scr.shape[0]
                    val_scr[slot] = val
                    for r in range(dil):
                        out_ref[kind, r, :, sl] = (
                            val_scr[slot, pl.ds(r, tm // dil, stride=dil), :].astype(out_ref.dtype))

            if kind == 2:
                for c in range(nchunk):
                    emit(c, acc[:, c * LANES:(c + 1) * LANES])
            else:
                qscale = HEAD_DIM ** -0.5 * LOG2E if kind == 0 else 1.0
                for c2 in range(COL // MXU_DIM):
                    sl2 = slice(c2 * MXU_DIM, (c2 + 1) * MXU_DIM)
                    a2 = acc[:, sl2]
                    ssq = _dot((a2 * a2).astype(BF16), ones)
                    xn = a2 * lax.rsqrt(ssq * (1.0 / HEAD_DIM) + EPS) * (gain_ref[t] * qscale)[:, sl2]
                    for cc in range(MXU_DIM // LANES):
                        xc = xn[:, cc * LANES:(cc + 1) * LANES]
                        emit(c2 * (MXU_DIM // LANES) + cc,
                             xc * cos + pltpu.roll(xc, HEAD_DIM, 1) * sin)


def split_w_in(w_in, qn_g, kn_g):
    depth, D, _ = w_in.shape
    seg, total = _w_in_segments(D)
    assert total == w_in.shape[2] and 2 * COL == D
    cut = lambda name: w_in[:, :, seg[name][0]:seg[name][0] + seg[name][1]]
    w_main = jnp.concatenate([cut(n) for n in ('a_x', 'a_g', 'c_v', 'c_g', 'm_g', 'c_q', 'c_k')],
                             axis=2).astype(BF16)
    cols = []
    for g in range(N_GROUPS):
        cols += [_pair_permute_cols(cut(f'q{g}')), _pair_permute_cols(cut(f'k{g}')), cut(f'v{g}')]
    cols.append(cut('b_g'))
    w_att = jnp.concatenate(cols, axis=2).astype(BF16)
    tiled = lambda g: _pair_permute_cols(jnp.tile(g, (1, 1, HEADS_PER_GROUP)))
    gains = jnp.stack([tiled(qn_g), tiled(kn_g), jnp.ones((depth, N_GROUPS, COL), F32)], axis=2)
    return w_main, w_att, gains.reshape(depth, N_ATT_TILES, 1, COL)


def proj_att(hb, w_att, cos_t, sin_t, gains, *, layer, batch, tm):
    T, D = hb.shape
    S = T // batch
    tiles_per_batch = S // tm
    ntile = N_ATT_TILES + 1
    lane = np.arange(MXU_DIM)
    head = (lane // LANES) * 2 + (lane % HEAD_DIM) // HALF
    ones_bd = jnp.asarray(head[:, None] == head[None, :], BF16)

    def qkv_spec(g):
        dil = DIL_PATTERNS[g][1]
        assert tm % (dil * BF16_ROWS) == 0
        return pl.BlockSpec((None, 3, dil, tm // dil, COL),
                            lambda i: (i // tiles_per_batch, 0, 0, i % tiles_per_batch, 0))

    def qkv_shape(g):
        dil = DIL_PATTERNS[g][1]
        return jax.ShapeDtypeStruct((batch, 3, dil, S // dil, COL), BF16)

    const = lambda shape: pl.BlockSpec(shape, lambda i: (0,) * len(shape),
                                       pipeline_mode=pl.Buffered(1))
    layer_const = lambda shape: pl.BlockSpec((None,) + shape, lambda i: (layer,) + (0,) * len(shape),
                                             pipeline_mode=pl.Buffered(1))
    return pl.pallas_call(
        _proj_att_kernel,
        grid=(T // tm,),
        in_specs=[pl.BlockSpec((tm, D), lambda i: (i, 0)),
                  layer_const((D, ntile * COL)),
                  pl.BlockSpec((tm, LANES), lambda i: (i, 0)),
                  pl.BlockSpec((tm, LANES), lambda i: (i, 0)),
                  layer_const((N_ATT_TILES, 1, COL)),
                  const((MXU_DIM, MXU_DIM))],
        out_specs=[qkv_spec(0), qkv_spec(1), qkv_spec(2),
                   pl.BlockSpec((tm, COL), lambda i: (i, 0))],
        out_shape=[qkv_shape(0), qkv_shape(1), qkv_shape(2),
                   jax.ShapeDtypeStruct((T, COL), BF16)],
        scratch_shapes=[pltpu.VMEM((2 * (COL // LANES), tm, LANES), F32)],
        compiler_params=_params(("arbitrary",)),
        name="proj_att",
    )(hb, w_att, cos_t, sin_t, gains, ones_bd)


def _lru_tile(xc_ref, wa_ref, ba_ref, wx_ref, bx_ref, lam_ref, a_scr, u_scr, hc_scr):
    ts, W = xc_ref.shape
    neg_c_sp = -LRU_C * _softplus(-lam_ref[...])
    for g in range(W // MXU_DIM):
        sl = slice(g * MXU_DIM, (g + 1) * MXU_DIM)
        xb = xc_ref[:, sl]
        xg = xb.astype(F32)
        r = _sigmoid(_dot(xb, wa_ref[g]) + ba_ref[:, sl])
        i = _sigmoid(_dot(xb, wx_ref[g]) + bx_ref[:, sl])
        log_a = neg_c_sp[:, sl] * r
        a = jnp.exp(log_a)
        a_scr[:, sl] = a
        y = jnp.tanh(-log_a) * (1.0 + a * a)
        mult = jnp.where(y > 0.0, y * lax.rsqrt(y), 0.0)
        u_scr[:, sl] = mult * (i * xg)

    row = lax.broadcasted_iota(jnp.int32, (SUBLANES, W), 0)
    h_prev = hc_scr[...]
    for t in range(ts // SUBLANES):
        rs = slice(t * SUBLANES, (t + 1) * SUBLANES)
        a = a_scr[rs, :]
        u = u_scr[rs, :]
        d = 1
        while d < SUBLANES:
            keep = row >= d
            a_sh = jnp.where(keep, pltpu.roll(a, d, 0), 1.0)
            u_sh = jnp.where(keep, pltpu.roll(u, d, 0), 0.0)
            u = a * u_sh + u
            a = a * a_sh
            d *= 2
        h = a * h_prev + u
        u_scr[rs, :] = h
        h_prev = jnp.broadcast_to(h[SUBLANES - 1:SUBLANES, :], (SUBLANES, W))
    hc_scr[...] = h_prev


def lru_block_diag(w):
    depth, nblk, bw, _ = w.shape
    per = MXU_DIM // bw
    w5 = w.reshape(depth, nblk // per, per, bw, bw)
    eye = jnp.eye(per, dtype=w.dtype)
    return jnp.einsum('lgpjk,pq->lgpjqk', w5, eye).reshape(depth, nblk // per, MXU_DIM, MXU_DIM).astype(BF16)


def _attn_kernel(q_ref, kp_ref, kc_ref, vp_ref, vc_ref, o_ref, lse_ref, s_scr, p_scr, *, span, tq):
    n = pl.program_id(2)
    blk = BAND_BLK
    nqb = tq // blk
    npair = COL // LANES
    qi = lax.broadcasted_iota(jnp.int32, (blk, 2 * blk), 0)
    kj = lax.broadcasted_iota(jnp.int32, (blk, 2 * blk), 1)
    rel = qi + blk - kj
    band = (rel >= 0) & (rel <= span)
    bias = jnp.where(band, 0.0, NEG_BIG).astype(F32)
    bias_first = jnp.where(band & ((kj >= blk) | (n > 0)), 0.0, NEG_BIG).astype(F32)
    k = jnp.concatenate([kp_ref[...], kc_ref[...]], axis=0)
    v = jnp.concatenate([vp_ref[...], vc_ref[...]], axis=0)
    lane_k = lax.broadcasted_iota(jnp.int32, (blk + tq, LANES), 1)
    lane_q = lax.broadcasted_iota(jnp.int32, (tq, LANES), 1)
    lane = lax.broadcasted_iota(jnp.int32, (blk, LANES), 1)
    lanes_per_head = LANES // HEADS_PER_GROUP
    zero = jnp.zeros((), BF16)
    ones_k = jnp.ones((2 * blk, LANES), BF16)
    tile = lambda c, qb, hh: (c * nqb + qb) * 2 + hh

    for c in range(npair):
        sl = slice(c * LANES, (c + 1) * LANES)
        q_c = q_ref[:, sl]
        k_c = k[:, sl]
        q_heads = (jnp.where((lane_q % HEAD_DIM) < HALF, q_c, zero),
                   jnp.where((lane_q % HEAD_DIM) >= HALF, q_c, zero))
        for qb in range(nqb):
            for hh in range(2):
                s = _dot_nt(q_heads[hh][qb * blk:(qb + 1) * blk], k_c[qb * blk:(qb + 2) * blk])
                s_scr[tile(c, qb, hh)] = s + (bias_first if qb == 0 else bias)

    l_pair = {}
    m_tiles = [jnp.zeros((blk, LANES), F32) for _ in range(nqb)]
    l_tiles = [jnp.ones((blk, LANES), F32) for _ in range(nqb)]
    for c in range(npair):
        for qb in range(nqb):
            l_heads = []
            for hh in range(2):
                t = tile(c, qb, hh)
                s = s_scr[t]
                m = jnp.max(s, axis=-1, keepdims=True)
                pb = jnp.exp2(s - m).astype(BF16)
                p_scr[t] = pb
                l = _dot(pb, ones_k)
                l_heads.append(l)
                head_lanes = lane // lanes_per_head == 2 * c + hh
                m_tiles[qb] = jnp.where(head_lanes, m, m_tiles[qb])
                l_tiles[qb] = jnp.where(head_lanes, l, l_tiles[qb])
            l_pair[(c, qb)] = jnp.where(lane < HEAD_DIM, l_heads[0], l_heads[1])

    for c in range(npair):
        sl = slice(c * LANES, (c + 1) * LANES)
        v_c = v[:, sl]
        v_heads = (jnp.where(lane_k < HEAD_DIM, v_c, zero), jnp.where(lane_k >= HEAD_DIM, v_c, zero))
        for qb in range(nqb):
            ks = slice(qb * blk, (qb + 2) * blk)
            o_pair = (_dot(p_scr[tile(c, qb, 0)], v_heads[0][ks])
                      + _dot(p_scr[tile(c, qb, 1)], v_heads[1][ks]))
            o_ref[qb * blk:(qb + 1) * blk, sl] = (o_pair / l_pair[(c, qb)]).astype(o_ref.dtype)
    for qb in range(nqb):
        lse_ref[qb * blk:(qb + 1) * blk, :] = m_tiles[qb] * LN2 + jnp.log(l_tiles[qb])


def attn_group(qkv, g, window, *, tq):
    B, _, dil, L, _ = qkv.shape
    tq = min(tq, L)
    assert L % tq == 0 and tq % BAND_BLK == 0
    per = tq // BAND_BLK
    cur = lambda kind: pl.BlockSpec((None, None, None, tq, COL), lambda b, r, n: (b, kind, r, n, 0))
    prev = lambda kind: pl.BlockSpec((None, None, None, BAND_BLK, COL),
                                     lambda b, r, n: (b, kind, r, jnp.maximum(n * per - 1, 0), 0))
    return pl.pallas_call(
        functools.partial(_attn_kernel, span=window // dil, tq=tq),
        grid=(B, dil, L // tq),
        in_specs=[cur(0), prev(1), cur(1), prev(2), cur(2)],
        out_specs=[pl.BlockSpec((None, None, tq, COL), lambda b, r, n: (b, r, n, 0)),
                   pl.BlockSpec((None, None, tq, LANES), lambda b, r, n: (b, r, n, 0))],
        out_shape=[jax.ShapeDtypeStruct((B, dil, L, COL), BF16),
                   jax.ShapeDtypeStruct((B, dil, L, LANES), F32)],
        scratch_shapes=[pltpu.VMEM((HEADS_PER_GROUP * per, BAND_BLK, 2 * BAND_BLK), F32),
                        pltpu.VMEM((HEADS_PER_GROUP * per, BAND_BLK, 2 * BAND_BLK), BF16)],
        compiler_params=_params(("arbitrary", "arbitrary", "arbitrary")),
        name=f"attn_group{g}",
    )(qkv, qkv, qkv, qkv, qkv)


def _gla_kernel(q_ref, k_ref, v_ref, g_ref, la_ref, og_ref, y_ref, st_scr, *, ts):
    s = pl.program_id(1)
    C = GLA_CHUNK
    kw = q_ref.shape[-1]
    dk = kw // GLA_HEADS
    dv = v_ref.shape[-1] // GLA_HEADS

    @pl.when(s == 0)
    def _():
        st_scr[...] = jnp.zeros_like(st_scr)

    row = lax.broadcasted_iota(jnp.int32, (C, kw), 0)
    qi = lax.broadcasted_iota(jnp.int32, (C, C), 0)
    kj = lax.broadcasted_iota(jnp.int32, (C, C), 1)
    causal = qi >= kj
    for c in range(ts // C):
        rs = slice(c * C, (c + 1) * C)
        b = la_ref[rs, :]
        d = 1
        while d < C:
            b = b + jnp.where(row >= d, pltpu.roll(b, d, 0), 0.0)
            d *= 2
        b_last = b[C - 1:C, :]
        qf = q_ref[rs, :].astype(F32)
        kf = k_ref[rs, :].astype(F32)
        q_dec = (qf * (dk ** -0.5) * jnp.exp(b)).astype(BF16)
        k_inv = (kf * jnp.exp(-b)).astype(BF16)
        k_end = (kf * jnp.exp(b_last - b)).astype(BF16)
        dec = jnp.exp(b_last)
        for h in range(GLA_HEADS):
            ks = slice(h * dk, (h + 1) * dk)
            vs = slice(h * dv, (h + 1) * dv)
            vh = v_ref[rs, vs]
            att = jnp.where(causal, _dot_nt(q_dec[:, ks], k_inv[:, ks]), 0.0)
            st = st_scr[h]
            o = _dot(att.astype(BF16), vh) + _dot_nt(q_dec[:, ks], st.astype(BF16))
            st_scr[h] = st * dec[:, ks] + _dot_tn(vh, k_end[:, ks])
            ms = jnp.mean(o * o, axis=-1, keepdims=True)
            yn = o * lax.rsqrt(ms + EPS) * og_ref[...]
            y_ref[rs, vs] = (yn * g_ref[rs, vs].astype(F32)).astype(y_ref.dtype)


def gla_branch(z3, la3, gla_on_g, *, layer, ts):
    B, S, _ = z3.shape
    kw = la3.shape[-1]
    dv = gla_on_g.shape[-1]
    vw = GLA_HEADS * dv
    dk = kw // GLA_HEADS
    assert vw == 2 * kw
    zmap = lambda width, col: pl.BlockSpec((None, ts, width), lambda b, s: (b, s, col))
    return pl.pallas_call(
        functools.partial(_gla_kernel, ts=ts),
        grid=(B, S // ts),
        in_specs=[zmap(kw, 2 * Z_CQK), zmap(kw, 2 * Z_CQK + 1), zmap(vw, Z_CV), zmap(vw, Z_CG),
                  pl.BlockSpec((None, ts, kw), lambda b, s: (b, s, 0)),
                  pl.BlockSpec((None, 1, dv), lambda b, s: (layer, 0, 0))],
        out_specs=pl.BlockSpec((None, ts, vw), lambda b, s: (b, s, 0)),
        out_shape=jax.ShapeDtypeStruct((B, S, vw), BF16),
        scratch_shapes=[pltpu.VMEM((GLA_HEADS, dv, dk), F32)],
        compiler_params=_params(("arbitrary", "arbitrary")),
        name="gla_branch",
    )(z3, z3, z3, z3, la3, gla_on_g)


def _merge_kernel(x_ref, gate_ref, xc_ref, ag_ref, o0_ref, o1_ref, o2_ref, l0_ref, l1_ref, l2_ref,
                  bg_ref, yc_ref, m0_ref, m1_ref, m2_ref, wa_ref, ba_ref, wx_ref, bx_ref, lam_ref,
                  pa_ref, pb_ref, pc_ref, wo_ref, out_ref,
                  o_scr, l_scr, a_scr, u_scr, hc_scr, *, tiles_per_batch):
    tm = x_ref.shape[0]

    @pl.when(pl.program_id(0) % tiles_per_batch == 0)
    def _():
        hc_scr[...] = jnp.zeros_like(hc_scr)

    _lru_tile(xc_ref, wa_ref, ba_ref, wx_ref, bx_ref, lam_ref, a_scr, u_scr, hc_scr)

    for g, (o_ref, l_ref) in enumerate(((o0_ref, l0_ref), (o1_ref, l1_ref), (o2_ref, l2_ref))):
        dil = DIL_PATTERNS[g][1]
        for r in range(dil):
            rows = pl.ds(r, tm // dil, stride=dil) if dil > 1 else slice(None)
            l_scr[g, rows, :] = l_ref[r]
            for c in range(ATT_OUT // LANES):
                o_scr[g, c, rows, :] = o_ref[r, :, c * LANES:(c + 1) * LANES].astype(F32)
    l0, l1, l2 = l_scr[0], l_scr[1], l_scr[2]
    mx = jnp.maximum(jnp.maximum(l0, l1), l2)
    e0, e1, e2 = jnp.exp(l0 - mx), jnp.exp(l1 - mx), jnp.exp(l2 - mx)
    inv_den = 1.0 / (e0 + e1 + e2)
    wts = (e0 * inv_den, e1 * inv_den, e2 * inv_den)
    lanes_per_head = LANES // HEADS_PER_GROUP
    lane = lax.broadcasted_iota(jnp.int32, (tm, LANES), 1)
    upper = lane >= HEAD_DIM
    cols = []
    for c in range(ATT_OUT // LANES):
        acc = jnp.zeros((tm, LANES), F32)
        for g, w in enumerate(wts):
            h0 = 2 * c * lanes_per_head
            h1 = (2 * c + 1) * lanes_per_head
            wv = jnp.where(upper, w[:, h1:h1 + 1], w[:, h0:h0 + 1])
            acc = acc + wv * o_scr[g, c]
        cols.append(acc)
    yb = (jnp.concatenate(cols, axis=-1) * bg_ref[...].astype(F32)).astype(BF16)
    half = tm // 2
    for r in range(2):
        rs = slice(r * half, (r + 1) * half)
        ya = (u_scr[rs, :] * ag_ref[rs, :].astype(F32)).astype(BF16)
        merged = (m0_ref[rs, :].astype(F32) * _dot(ya, pa_ref[...])
                  + m1_ref[rs, :].astype(F32) * _dot(yb[rs], pb_ref[...])
                  + m2_ref[rs, :].astype(F32) * _dot(yc_ref[rs, :], pc_ref[...]))
        out_ref[rs, :] = x_ref[rs, :] + gate_ref[...] * _dot(merged.astype(BF16), wo_ref[...])


def merge_layer(x2, mods, o_groups, lse_groups, bg, yc, z2, lru_params, proj_a, proj_b, proj_c,
                w_o, *, layer, batch, tm):
    T, D = x2.shape
    tiles_per_batch = T // batch // tm
    rows = lambda width, col=0: pl.BlockSpec((tm, width), lambda i: (i, col))
    full = lambda a: pl.BlockSpec((None,) + a.shape[1:], lambda i: (layer,) + (0,) * (a.ndim - 1),
                                  pipeline_mode=pl.Buffered(1))

    def residue_major(g, width):
        dil = DIL_PATTERNS[g][1]
        assert tm % (dil * BF16_ROWS) == 0
        return pl.BlockSpec((None, dil, tm // dil, width),
                            lambda i: (i // tiles_per_batch, 0, i % tiles_per_batch, 0))

    weights = (proj_a, proj_b, proj_c, w_o)
    gate_spec = pl.BlockSpec((None, None, None, 1, D), lambda i: (layer, i // tiles_per_batch, 2, 0, 0))
    return pl.pallas_call(
        functools.partial(_merge_kernel, tiles_per_batch=tiles_per_batch),
        grid=(T // tm,),
        in_specs=[rows(D), gate_spec, rows(D, Z_AX), rows(D, Z_AG)]
                 + [residue_major(g, COL) for g in range(N_GROUPS)]
                 + [residue_major(g, LANES) for g in range(N_GROUPS)]
                 + [rows(COL), rows(D), rows(D, Z_MG), rows(D, Z_MG + 1), rows(D, Z_MG + 2)]
                 + [full(p) for p in lru_params] + [full(w) for w in weights],
        out_specs=rows(D),
        out_shape=jax.ShapeDtypeStruct((T, D), F32),
        scratch_shapes=[pltpu.VMEM((N_GROUPS, ATT_OUT // LANES, tm, LANES), F32),
                        pltpu.VMEM((N_GROUPS, tm, LANES), F32),
                        pltpu.VMEM((tm, D), F32), pltpu.VMEM((tm, D), F32),
                        pltpu.VMEM((SUBLANES, D), F32)],
        compiler_params=_params(("arbitrary",)),
        name="merge_layer",
    )(x2, mods, z2, z2, *o_groups, *lse_groups, bg, yc, z2, z2, z2, *lru_params, *weights)


def kernel(x, c, positions, ada_w, ada_b, norm_g, w_in, conv_w, conv_b, lru_wa, lru_ba, lru_wx,
           lru_bx, lru_lambda, qn_g, kn_g, gla_a1, gla_a2, gla_ab, gla_on_g, proj_a, proj_b,
           proj_c, w_o):
    B, S, D = x.shape
    T = B * S
    depth = ada_w.shape[0]
    tm_main = min(512, S)
    tm_att = min(1024, S)
    tm_merge = min(512, S)
    ts_seq = min(512, S)
    tq_att = 512
    row = lambda v: v[:, None, :]
    w_main, w_att, gains = split_w_in(w_in, qn_g, kn_g)
    a1p = jnp.pad(gla_a1, ((0, 0), (0, 0), (0, LANES - GLA_RANK))).astype(BF16)
    a2p = jnp.pad(gla_a2, ((0, 0), (0, LANES - GLA_RANK), (0, 0))).astype(BF16)
    lru_params = (lru_block_diag(lru_wa), row(lru_ba), lru_block_diag(lru_wx), row(lru_bx),
                  row(lru_lambda))
    out_w = [w.astype(BF16) for w in (proj_a, proj_b, proj_c, w_o)]

    cos_t, sin_t = rope_tables(positions)
    mods = adaln_mod(c, ada_w, ada_b).reshape(depth, SUBLANES, 3, 1, D)
    x2 = x.reshape(T, D)
    for l in range(depth):
        z2, la, hb = proj_main(x2, mods, row(norm_g), w_main, a1p, a2p, row(gla_ab), conv_w,
                               row(conv_b), layer=l, batch=B, tm=tm_main)
        *qkv, bg = proj_att(hb, w_att, cos_t, sin_t, gains, layer=l, batch=B, tm=tm_att)
        z3 = z2.reshape(B, S, -1)
        outs = [attn_group(qkv[g], g, win, tq=tq_att) for g, (win, _) in enumerate(DIL_PATTERNS)]
        yc = gla_branch(z3, la.reshape(B, S, -1), row(gla_on_g), layer=l, ts=ts_seq)
        x2 = merge_layer(x2, mods, [o for o, _ in outs], [s for _, s in outs], bg,
                         yc.reshape(T, -1), z2, lru_params, *out_w, layer=l, batch=B, tm=tm_merge)
    return x2.reshape(B, S, D)
```

```python
import functools

import jax
import jax.numpy as jnp
import numpy as np
from jax import lax
from jax.experimental import pallas as pl
from jax.experimental.pallas import tpu as pltpu

LRU_BLOCKS = 16
CONV_W = 4
LRU_C = 8.0
DIL_PATTERNS = ((128, 1), (512, 4), (2048, 16))
N_GROUPS = 3
HEADS_PER_GROUP = 8
HEAD_DIM = 64
HALF = HEAD_DIM // 2
ATT_OUT = HEADS_PER_GROUP * HEAD_DIM
BAND_BLK = 128
ROPE_THETA = 10000.0
GLA_HEADS = 4
GLA_RANK = 16
GLA_NORMALIZER = 16.0
GLA_CHUNK = 64
EPS = 1e-6

LANES = 128
SUBLANES = 8
BF16_ROWS = 16
MXU_DIM = 256
VMEM_LIMIT = 48 * 1024 * 1024
VMEM_LIMIT_MAIN = 56 * 1024 * 1024

COL = 512
N_ATT_TILES = 3 * N_GROUPS
NEG_BIG = -1e30
LOG2E = float(np.log2(np.e))
LN2 = float(np.log(2.0))
BF16 = jnp.bfloat16
F32 = jnp.float32


def _params(sem, vmem_limit=VMEM_LIMIT):
    return pltpu.CompilerParams(dimension_semantics=sem, vmem_limit_bytes=vmem_limit)


def _split_bf16(x):
    hi = x.astype(BF16)
    lo = (x - hi.astype(F32)).astype(BF16)
    return hi, lo


def _dot(a, b):
    return jnp.dot(a, b, preferred_element_type=F32)


def _dot_nt(a, b):
    return lax.dot_general(a, b, (((1,), (1,)), ((), ())), preferred_element_type=F32)


def _dot_tn(a, b):
    return lax.dot_general(a, b, (((0,), (0,)), ((), ())), preferred_element_type=F32)


def _sigmoid(x):
    return 0.5 * jnp.tanh(0.5 * x) + 0.5


def _silu(x):
    h = 0.5 * x
    return h + h * jnp.tanh(h)


def _softplus(x):
    return jnp.maximum(x, 0.0) + jnp.log1p(jnp.exp(-jnp.abs(x)))


def _pair_permute_cols(w):
    shp = w.shape
    w5 = w.reshape(shp[:-1] + (shp[-1] // LANES, 2, 2, HALF))
    return jnp.swapaxes(w5, -3, -2).reshape(shp)


def _rope_kernel(pos_ref, inv_ref, sgn_ref, cos_ref, sin_ref):
    ang = pos_ref[...].astype(F32) * inv_ref[...]
    cos_ref[...] = jnp.cos(ang)
    sin_ref[...] = jnp.sin(ang) * sgn_ref[...]


def rope_tables(positions):
    T = positions.size
    inv = ROPE_THETA ** (-(np.arange(HALF, dtype=np.float32) / HALF))
    lane = np.arange(LANES)
    inv_row = jnp.asarray(inv[lane % HALF].reshape(1, LANES), F32)
    sgn_row = jnp.asarray(np.where(lane < HEAD_DIM, -1.0, 1.0).reshape(1, LANES), F32)
    rows = min(T, 2048)
    return pl.pallas_call(
        _rope_kernel,
        grid=(T // rows,),
        in_specs=[pl.BlockSpec((rows, 1), lambda i: (i, 0)),
                  pl.BlockSpec((1, LANES), lambda i: (0, 0)),
                  pl.BlockSpec((1, LANES), lambda i: (0, 0))],
        out_specs=[pl.BlockSpec((rows, LANES), lambda i: (i, 0)),
                   pl.BlockSpec((rows, LANES), lambda i: (i, 0))],
        out_shape=[jax.ShapeDtypeStruct((T, LANES), F32)] * 2,
        compiler_params=_params(("arbitrary",)),
        name="rope_tables",
    )(positions.reshape(T, 1), inv_row, sgn_row)


def _mod_kernel(c_ref, w_ref, b_ref, o_ref):
    ca = _silu(c_ref[...])
    c_hi, c_lo = _split_bf16(ca)
    w_hi, w_lo = _split_bf16(w_ref[...])
    acc = _dot(c_hi, w_hi) + _dot(c_lo, w_hi) + _dot(c_hi, w_lo)
    o_ref[...] = acc + b_ref[...]


def adaln_mod(c, ada_w, ada_b):
    depth, D, D3 = ada_w.shape
    B = c.shape[0]
    cp = jnp.pad(c, ((0, SUBLANES - B), (0, 0)))
    tn = 1024
    return pl.pallas_call(
        _mod_kernel,
        grid=(depth, D3 // tn),
        in_specs=[pl.BlockSpec((SUBLANES, D), lambda l, j: (0, 0)),
                  pl.BlockSpec((None, D, tn), lambda l, j: (l, 0, j)),
                  pl.BlockSpec((None, 1, tn), lambda l, j: (l, 0, j))],
        out_specs=pl.BlockSpec((None, SUBLANES, tn), lambda l, j: (l, 0, j)),
        out_shape=jax.ShapeDtypeStruct((depth, SUBLANES, D3), F32),
        compiler_params=_params(("arbitrary", "arbitrary")),
        name="adaln_mod",
    )(cp, ada_w, ada_b.reshape(depth, 1, D3))


def _w_in_segments(D):
    att = N_GROUPS * ATT_OUT
    seg = {'a_x': (0, D), 'a_g': (D, D)}
    for g in range(N_GROUPS):
        seg[f'q{g}'] = (2 * D + g * ATT_OUT, ATT_OUT)
        seg[f'k{g}'] = (2 * D + att + g * ATT_OUT, ATT_OUT)
        seg[f'v{g}'] = (2 * D + 2 * att + g * ATT_OUT, ATT_OUT)
    off = 2 * D + 3 * att
    for name, width in (('b_g', ATT_OUT), ('c_q', COL), ('c_k', COL), ('c_v', D), ('c_g', D),
                        ('m_g', 3 * D)):
        seg[name] = (off, width)
        off += width
    return seg, off


Z_AX, Z_AG, Z_CV, Z_CG, Z_MG, Z_CQK = 0, 1, 2, 3, 4, 7
Z_TILES = 8


def _proj_main_kernel(x0_ref, xn_ref, ng_ref, scale0_ref, shift0_ref, scalen_ref, shiftn_ref,
                      w_ref, a1_ref, a2_ref, ab_ref, cw_ref, cb_ref,
                      z_ref, la_ref, hb_ref, h_scr, hn_scr, xe_scr, *, tiles_per_batch):
    i = pl.program_id(0)
    tm, D = hb_ref.shape

    def normed(x_ref, scale_ref, shift_ref):
        xf = x_ref[...]
        ms = jnp.mean(xf * xf, axis=-1, keepdims=True)
        y = xf * lax.rsqrt(ms + EPS) * ng_ref[...]
        return (y * (1.0 + scale_ref[...]) + shift_ref[...]).astype(BF16)

    @pl.when(i == 0)
    def _():
        h_scr[...] = normed(x0_ref, scale0_ref, shift0_ref)

    @pl.when(i > 0)
    def _():
        h_scr[...] = hn_scr[...]

    first_in_batch = i % tiles_per_batch == 0

    @pl.when(first_in_batch)
    def _():
        xe_scr[0:SUBLANES, :] = jnp.zeros((SUBLANES, D), F32)

    @pl.when(jnp.logical_not(first_in_batch))
    def _():
        xe_scr[0:SUBLANES, :] = xe_scr[tm:tm + SUBLANES, :]

    hb = h_scr[...]
    hb_ref[...] = hb
    t = _dot(hb, a1_ref[...]).astype(BF16)
    pre = _dot(t, a2_ref[...]) + ab_ref[...]
    la_ref[...] = -_softplus(-pre) * (1.0 / GLA_NORMALIZER)

    def tile(c):
        return _dot(hb, w_ref[:, c * D:(c + 1) * D])

    def put(c, val):
        z_ref[:, c * D:(c + 1) * D] = val.astype(z_ref.dtype)

    ax = tile(Z_AX)
    xe_scr[SUBLANES:SUBLANES + tm, :] = ax
    xc = cb_ref[...] + ax * cw_ref[CONV_W - 1:CONV_W, :]
    for k in range(CONV_W - 1):
        off = SUBLANES - (CONV_W - 1) + k
        xc = xc + xe_scr[off:off + tm, :] * cw_ref[k:k + 1, :]
    put(Z_AX, xc)
    put(Z_AG, _silu(tile(Z_AG)))
    put(Z_CV, tile(Z_CV))
    put(Z_CG, _silu(tile(Z_CG)))
    for c in range(3):
        put(Z_MG + c, _sigmoid(tile(Z_MG + c)))
    put(Z_CQK, tile(Z_CQK))
    hn_scr[...] = normed(xn_ref, scalen_ref, shiftn_ref)


def proj_main(x2, mods, norm_g, w_main, a1p, a2p, gla_ab, conv_w, conv_b, *, layer, batch, tm):
    T, D = x2.shape
    kw = a2p.shape[-1]
    ntile = T // tm
    tiles_per_batch = ntile // batch
    nxt = lambda i: jnp.minimum(i + 1, ntile - 1)
    const = lambda shape: pl.BlockSpec(shape, lambda i: (0,) * len(shape),
                                       pipeline_mode=pl.Buffered(1))
    layer_const = lambda shape: pl.BlockSpec((None,) + shape, lambda i: (layer,) + (0,) * len(shape),
                                             pipeline_mode=pl.Buffered(1))
    mod_spec = lambda which, row: pl.BlockSpec((None, None, None, 1, D),
                                               lambda i: (layer, row(i), which, 0, 0))
    return pl.pallas_call(
        functools.partial(_proj_main_kernel, tiles_per_batch=tiles_per_batch),
        grid=(ntile,),
        in_specs=[pl.BlockSpec((tm, D), lambda i: (0, 0)),
                  pl.BlockSpec((tm, D), lambda i: (nxt(i), 0)),
                  layer_const((1, D)),
                  mod_spec(1, lambda i: 0), mod_spec(0, lambda i: 0),
                  mod_spec(1, lambda i: nxt(i) // tiles_per_batch),
                  mod_spec(0, lambda i: nxt(i) // tiles_per_batch),
                  layer_const((D, Z_TILES * D)),
                  layer_const((D, LANES)), layer_const((LANES, kw)), layer_const((1, kw)),
                  layer_const((CONV_W, D)), layer_const((1, D))],
        out_specs=[pl.BlockSpec((tm, Z_TILES * D), lambda i: (i, 0)),
                   pl.BlockSpec((tm, kw), lambda i: (i, 0)),
                   pl.BlockSpec((tm, D), lambda i: (i, 0))],
        out_shape=[jax.ShapeDtypeStruct((T, Z_TILES * D), BF16),
                   jax.ShapeDtypeStruct((T, kw), F32),
                   jax.ShapeDtypeStruct((T, D), BF16)],
        scratch_shapes=[pltpu.VMEM((tm, D), BF16), pltpu.VMEM((tm, D), BF16),
                        pltpu.VMEM((tm + SUBLANES, D), F32)],
        compiler_params=_params(("arbitrary",), VMEM_LIMIT_MAIN),
        name="proj_main",
    )(x2, x2, norm_g, mods, mods, mods, mods, w_main, a1p, a2p, gla_ab, conv_w, conv_b)


def _proj_att_kernel(h_ref, w_ref, cos_ref, sin_ref, gain_ref, ones_ref,
                     cq_ref, ck_ref, cv_ref, cg_ref, la_ref, og_ref,
                     qkv0_ref, qkv1_ref, qkv2_ref, bg_ref, yc_ref, val_scr, st_scr,
                     *, tiles_per_batch):
    tm = h_ref.shape[0]
    nchunk = COL // LANES

    @pl.when(pl.program_id(0) % tiles_per_batch == 0)
    def _():
        st_scr[...] = jnp.zeros_like(st_scr)

    _gla_tile(cq_ref, ck_ref, cv_ref, cg_ref, la_ref, og_ref, yc_ref, st_scr)

    h = h_ref[...]
    ones = ones_ref[...]
    cos = cos_ref[...]
    sin = sin_ref[...]
    bg_ref[...] = _silu(_dot(h, w_ref[:, N_ATT_TILES * COL:])).astype(bg_ref.dtype)
    for g, out_ref in enumerate((qkv0_ref, qkv1_ref, qkv2_ref)):
        dil = DIL_PATTERNS[g][1]
        for kind in range(3):
            t = 3 * g + kind
            acc = _dot(h, w_ref[:, t * COL:(t + 1) * COL])

            def emit(c, val, t=t, kind=kind, dil=dil, out_ref=out_ref):
                sl = slice(c * LANES, (c + 1) * LANES)
                if dil == 1:
                    out_ref[kind, 0, :, sl] = val.astype(out_ref.dtype)
                else:
                    slot = (t * nchunk + c) % val_scr.shape[0]
                    val_scr[slot] = val
                    for r in range(dil):
                        out_ref[kind, r, :, sl] = (
                            val_scr[slot, pl.ds(r, tm // dil, stride=dil), :].astype(out_ref.dtype))

            if kind == 2:
                for c in range(nchunk):
                    emit(c, acc[:, c * LANES:(c + 1) * LANES])
            else:
                qscale = HEAD_DIM ** -0.5 * LOG2E if kind == 0 else 1.0
                for c2 in range(COL // MXU_DIM):
                    sl2 = slice(c2 * MXU_DIM, (c2 + 1) * MXU_DIM)
                    a2 = acc[:, sl2]
                    ssq = _dot((a2 * a2).astype(BF16), ones)
                    xn = a2 * lax.rsqrt(ssq * (1.0 / HEAD_DIM) + EPS) * (gain_ref[t] * qscale)[:, sl2]
                    for cc in range(MXU_DIM // LANES):
                        xc = xn[:, cc * LANES:(cc + 1) * LANES]
                        emit(c2 * (MXU_DIM // LANES) + cc,
                             xc * cos + pltpu.roll(xc, HEAD_DIM, 1) * sin)


def split_w_in(w_in, qn_g, kn_g):
    depth, D, _ = w_in.shape
    seg, total = _w_in_segments(D)
    assert total == w_in.shape[2] and 2 * COL == D
    cut = lambda name: w_in[:, :, seg[name][0]:seg[name][0] + seg[name][1]]
    w_main = jnp.concatenate([cut(n) for n in ('a_x', 'a_g', 'c_v', 'c_g', 'm_g', 'c_q', 'c_k')],
                             axis=2).astype(BF16)
    cols = []
    for g in range(N_GROUPS):
        cols += [_pair_permute_cols(cut(f'q{g}')), _pair_permute_cols(cut(f'k{g}')), cut(f'v{g}')]
    cols.append(cut('b_g'))
    w_att = jnp.concatenate(cols, axis=2).astype(BF16)
    tiled = lambda g: _pair_permute_cols(jnp.tile(g, (1, 1, HEADS_PER_GROUP)))
    gains = jnp.stack([tiled(qn_g), tiled(kn_g), jnp.ones((depth, N_GROUPS, COL), F32)], axis=2)
    return w_main, w_att, gains.reshape(depth, N_ATT_TILES, 1, COL)


def proj_att(hb, w_att, cos_t, sin_t, gains, z2, la, gla_on_g, *, layer, batch, tm):
    T, D = hb.shape
    S = T // batch
    tiles_per_batch = S // tm
    ntile = N_ATT_TILES + 1
    kw = la.shape[-1]
    dv = gla_on_g.shape[-1]
    assert kw == COL and GLA_HEADS * dv == D and tm % GLA_CHUNK == 0
    rows = lambda width, col=0: pl.BlockSpec((tm, width), lambda i: (i, col))
    lane = np.arange(MXU_DIM)
    head = (lane // LANES) * 2 + (lane % HEAD_DIM) // HALF
    ones_bd = jnp.asarray(head[:, None] == head[None, :], BF16)

    def qkv_spec(g):
        dil = DIL_PATTERNS[g][1]
        assert tm % (dil * BF16_ROWS) == 0
        return pl.BlockSpec((None, 3, dil, tm // dil, COL),
                            lambda i: (i // tiles_per_batch, 0, 0, i % tiles_per_batch, 0))

    def qkv_shape(g):
        dil = DIL_PATTERNS[g][1]
        return jax.ShapeDtypeStruct((batch, 3, dil, S // dil, COL), BF16)

    const = lambda shape: pl.BlockSpec(shape, lambda i: (0,) * len(shape),
                                       pipeline_mode=pl.Buffered(1))
    layer_const = lambda shape: pl.BlockSpec((None,) + shape, lambda i: (layer,) + (0,) * len(shape),
                                             pipeline_mode=pl.Buffered(1))
    return pl.pallas_call(
        functools.partial(_proj_att_kernel, tiles_per_batch=tiles_per_batch),
        grid=(T // tm,),
        in_specs=[rows(D),
                  layer_const((D, ntile * COL)),
                  rows(LANES), rows(LANES),
                  layer_const((N_ATT_TILES, 1, COL)),
                  const((MXU_DIM, MXU_DIM)),
                  rows(kw, 2 * Z_CQK), rows(kw, 2 * Z_CQK + 1), rows(D, Z_CV), rows(D, Z_CG),
                  rows(kw), layer_const((1, dv))],
        out_specs=[qkv_spec(0), qkv_spec(1), qkv_spec(2), rows(COL), rows(D)],
        out_shape=[qkv_shape(0), qkv_shape(1), qkv_shape(2),
                   jax.ShapeDtypeStruct((T, COL), BF16), jax.ShapeDtypeStruct((T, D), BF16)],
        scratch_shapes=[pltpu.VMEM((2 * (COL // LANES), tm, LANES), F32),
                        pltpu.VMEM((GLA_HEADS, dv, kw // GLA_HEADS), F32)],
        compiler_params=_params(("arbitrary",)),
        name="proj_att",
    )(hb, w_att, cos_t, sin_t, gains, ones_bd, z2, z2, z2, z2, la, gla_on_g)


def _lru_tile(xc_ref, wa_ref, ba_ref, wx_ref, bx_ref, lam_ref, a_scr, u_scr, hc_scr):
    ts, W = xc_ref.shape
    neg_c_sp = -LRU_C * _softplus(-lam_ref[...])
    for g in range(W // MXU_DIM):
        sl = slice(g * MXU_DIM, (g + 1) * MXU_DIM)
        xb = xc_ref[:, sl]
        xg = xb.astype(F32)
        r = _sigmoid(_dot(xb, wa_ref[g]) + ba_ref[:, sl])
        i = _sigmoid(_dot(xb, wx_ref[g]) + bx_ref[:, sl])
        log_a = neg_c_sp[:, sl] * r
        a = jnp.exp(log_a)
        a_scr[:, sl] = a
        y = jnp.tanh(-log_a) * (1.0 + a * a)
        mult = jnp.where(y > 0.0, y * lax.rsqrt(y), 0.0)
        u_scr[:, sl] = mult * (i * xg)

    row = lax.broadcasted_iota(jnp.int32, (SUBLANES, W), 0)
    h_prev = hc_scr[...]
    for t in range(ts // SUBLANES):
        rs = slice(t * SUBLANES, (t + 1) * SUBLANES)
        a = a_scr[rs, :]
        u = u_scr[rs, :]
        d = 1
        while d < SUBLANES:
            keep = row >= d
            a_sh = jnp.where(keep, pltpu.roll(a, d, 0), 1.0)
            u_sh = jnp.where(keep, pltpu.roll(u, d, 0), 0.0)
            u = a * u_sh + u
            a = a * a_sh
            d *= 2
        h = a * h_prev + u
        u_scr[rs, :] = h
        h_prev = jnp.broadcast_to(h[SUBLANES - 1:SUBLANES, :], (SUBLANES, W))
    hc_scr[...] = h_prev


def lru_block_diag(w):
    depth, nblk, bw, _ = w.shape
    per = MXU_DIM // bw
    w5 = w.reshape(depth, nblk // per, per, bw, bw)
    eye = jnp.eye(per, dtype=w.dtype)
    return jnp.einsum('lgpjk,pq->lgpjqk', w5, eye).reshape(depth, nblk // per, MXU_DIM, MXU_DIM).astype(BF16)


def _attn_kernel(q_ref, kp_ref, kc_ref, vp_ref, vc_ref, o_ref, lse_ref, s_scr, p_scr, *, span, tq):
    n = pl.program_id(2)
    blk = BAND_BLK
    nqb = tq // blk
    npair = COL // LANES
    qi = lax.broadcasted_iota(jnp.int32, (blk, 2 * blk), 0)
    kj = lax.broadcasted_iota(jnp.int32, (blk, 2 * blk), 1)
    rel = qi + blk - kj
    band = (rel >= 0) & (rel <= span)
    bias = jnp.where(band, 0.0, NEG_BIG).astype(F32)
    bias_first = jnp.where(band & ((kj >= blk) | (n > 0)), 0.0, NEG_BIG).astype(F32)
    k = jnp.concatenate([kp_ref[...], kc_ref[...]], axis=0)
    v = jnp.concatenate([vp_ref[...], vc_ref[...]], axis=0)
    lane_k = lax.broadcasted_iota(jnp.int32, (blk + tq, LANES), 1)
    lane_q = lax.broadcasted_iota(jnp.int32, (tq, LANES), 1)
    lane = lax.broadcasted_iota(jnp.int32, (blk, LANES), 1)
    lanes_per_head = LANES // HEADS_PER_GROUP
    zero = jnp.zeros((), BF16)
    tile = lambda c, qb, hh: (c * nqb + qb) * 2 + hh

    for c in range(npair):
        sl = slice(c * LANES, (c + 1) * LANES)
        q_c = q_ref[:, sl]
        k_c = k[:, sl]
        q_heads = (jnp.where((lane_q % HEAD_DIM) < HALF, q_c, zero),
                   jnp.where((lane_q % HEAD_DIM) >= HALF, q_c, zero))
        for qb in range(nqb):
            for hh in range(2):
                s = _dot_nt(q_heads[hh][qb * blk:(qb + 1) * blk], k_c[qb * blk:(qb + 2) * blk])
                s_scr[tile(c, qb, hh)] = s + (bias_first if qb == 0 else bias)

    m_tiles = [jnp.zeros((blk, LANES), F32) for _ in range(nqb)]
    for c in range(npair):
        for qb in range(nqb):
            for hh in range(2):
                t = tile(c, qb, hh)
                s = s_scr[t]
                m = jnp.max(s, axis=-1, keepdims=True)
                p_scr[t] = jnp.exp2(s - m).astype(BF16)
                m_tiles[qb] = jnp.where(lane // lanes_per_head == 2 * c + hh, m, m_tiles[qb])

    l_tiles = [jnp.ones((blk, LANES), F32) for _ in range(nqb)]
    ones_k = jnp.ones((blk + tq, LANES), BF16)
    for c in range(npair):
        sl = slice(c * LANES, (c + 1) * LANES)
        v_c = v[:, sl]
        v_ext = (jnp.concatenate([jnp.where(lane_k < HEAD_DIM, v_c, zero), ones_k], axis=-1),
                 jnp.concatenate([jnp.where(lane_k >= HEAD_DIM, v_c, zero), ones_k], axis=-1))
        for qb in range(nqb):
            ks = slice(qb * blk, (qb + 2) * blk)
            oa = _dot(p_scr[tile(c, qb, 0)], v_ext[0][ks])
            ob = _dot(p_scr[tile(c, qb, 1)], v_ext[1][ks])
            la, lb = oa[:, LANES:], ob[:, LANES:]
            l_pair = jnp.where(lane < HEAD_DIM, la, lb)
            o_pair = oa[:, :LANES] + ob[:, :LANES]
            o_ref[qb * blk:(qb + 1) * blk, sl] = (o_pair / l_pair).astype(o_ref.dtype)
            l_tiles[qb] = jnp.where(lane // lanes_per_head == 2 * c, la,
                                    jnp.where(lane // lanes_per_head == 2 * c + 1, lb, l_tiles[qb]))
    for qb in range(nqb):
        lse_ref[qb * blk:(qb + 1) * blk, :] = m_tiles[qb] * LN2 + jnp.log(l_tiles[qb])


def attn_group(qkv, g, window, *, tq):
    B, _, dil, L, _ = qkv.shape
    tq = min(tq, L)
    assert L % tq == 0 and tq % BAND_BLK == 0
    per = tq // BAND_BLK
    cur = lambda kind: pl.BlockSpec((None, None, None, tq, COL), lambda b, r, n: (b, kind, r, n, 0))
    prev = lambda kind: pl.BlockSpec((None, None, None, BAND_BLK, COL),
                                     lambda b, r, n: (b, kind, r, jnp.maximum(n * per - 1, 0), 0))
    return pl.pallas_call(
        functools.partial(_attn_kernel, span=window // dil, tq=tq),
        grid=(B, dil, L // tq),
        in_specs=[cur(0), prev(1), cur(1), prev(2), cur(2)],
        out_specs=[pl.BlockSpec((None, None, tq, COL), lambda b, r, n: (b, r, n, 0)),
                   pl.BlockSpec((None, None, tq, LANES), lambda b, r, n: (b, r, n, 0))],
        out_shape=[jax.ShapeDtypeStruct((B, dil, L, COL), BF16),
                   jax.ShapeDtypeStruct((B, dil, L, LANES), F32)],
        scratch_shapes=[pltpu.VMEM((HEADS_PER_GROUP * per, BAND_BLK, 2 * BAND_BLK), F32),
                        pltpu.VMEM((HEADS_PER_GROUP * per, BAND_BLK, 2 * BAND_BLK), BF16)],
        compiler_params=_params(("arbitrary", "arbitrary", "arbitrary")),
        name=f"attn_group{g}",
    )(qkv, qkv, qkv, qkv, qkv)


def _gla_tile(q_ref, k_ref, v_ref, g_ref, la_ref, og_ref, y_ref, st_scr):
    C = GLA_CHUNK
    ts, kw = q_ref.shape
    dk = kw // GLA_HEADS
    dv = v_ref.shape[-1] // GLA_HEADS
    row = lax.broadcasted_iota(jnp.int32, (C, kw), 0)
    qi = lax.broadcasted_iota(jnp.int32, (C, C), 0)
    kj = lax.broadcasted_iota(jnp.int32, (C, C), 1)
    causal = qi >= kj
    for c in range(ts // C):
        rs = slice(c * C, (c + 1) * C)
        b = la_ref[rs, :]
        d = 1
        while d < C:
            b = b + jnp.where(row >= d, pltpu.roll(b, d, 0), 0.0)
            d *= 2
        b_last = b[C - 1:C, :]
        qf = q_ref[rs, :].astype(F32)
        kf = k_ref[rs, :].astype(F32)
        q_dec = (qf * (dk ** -0.5) * jnp.exp(b)).astype(BF16)
        k_inv = (kf * jnp.exp(-b)).astype(BF16)
        k_end = (kf * jnp.exp(b_last - b)).astype(BF16)
        dec = jnp.exp(b_last)
        for h in range(GLA_HEADS):
            ks = slice(h * dk, (h + 1) * dk)
            vs = slice(h * dv, (h + 1) * dv)
            vh = v_ref[rs, vs]
            att = jnp.where(causal, _dot_nt(q_dec[:, ks], k_inv[:, ks]), 0.0)
            st = st_scr[h]
            o = _dot(att.astype(BF16), vh) + _dot_nt(q_dec[:, ks], st.astype(BF16))
            st_scr[h] = st * dec[:, ks] + _dot_tn(vh, k_end[:, ks])
            ms = jnp.mean(o * o, axis=-1, keepdims=True)
            yn = o * lax.rsqrt(ms + EPS) * og_ref[...]
            y_ref[rs, vs] = (yn * g_ref[rs, vs].astype(F32)).astype(y_ref.dtype)


def _merge_kernel(x_ref, gate_ref, xc_ref, ag_ref, o0_ref, o1_ref, o2_ref, l0_ref, l1_ref, l2_ref,
                  bg_ref, yc_ref, m0_ref, m1_ref, m2_ref, wa_ref, ba_ref, wx_ref, bx_ref, lam_ref,
                  pa_ref, pb_ref, pc_ref, wo_ref, out_ref,
                  o_scr, l_scr, a_scr, u_scr, hc_scr, *, tiles_per_batch):
    tm = x_ref.shape[0]

    @pl.when(pl.program_id(0) % tiles_per_batch == 0)
    def _():
        hc_scr[...] = jnp.zeros_like(hc_scr)

    _lru_tile(xc_ref, wa_ref, ba_ref, wx_ref, bx_ref, lam_ref, a_scr, u_scr, hc_scr)

    for g, (o_ref, l_ref) in enumerate(((o0_ref, l0_ref), (o1_ref, l1_ref), (o2_ref, l2_ref))):
        dil = DIL_PATTERNS[g][1]
        for r in range(dil):
            rows = pl.ds(r, tm // dil, stride=dil) if dil > 1 else slice(None)
            l_scr[g, rows, :] = l_ref[r]
            for c in range(ATT_OUT // LANES):
                o_scr[g, c, rows, :] = o_ref[r, :, c * LANES:(c + 1) * LANES].astype(F32)
    l0, l1, l2 = l_scr[0], l_scr[1], l_scr[2]
    mx = jnp.maximum(jnp.maximum(l0, l1), l2)
    e0, e1, e2 = jnp.exp(l0 - mx), jnp.exp(l1 - mx), jnp.exp(l2 - mx)
    inv_den = 1.0 / (e0 + e1 + e2)
    wts = (e0 * inv_den, e1 * inv_den, e2 * inv_den)
    lanes_per_head = LANES // HEADS_PER_GROUP
    lane = lax.broadcasted_iota(jnp.int32, (tm, LANES), 1)
    upper = lane >= HEAD_DIM
    cols = []
    for c in range(ATT_OUT // LANES):
        acc = jnp.zeros((tm, LANES), F32)
        for g, w in enumerate(wts):
            h0 = 2 * c * lanes_per_head
            h1 = (2 * c + 1) * lanes_per_head
            wv = jnp.where(upper, w[:, h1:h1 + 1], w[:, h0:h0 + 1])
            acc = acc + wv * o_scr[g, c]
        cols.append(acc)
    yb = (jnp.concatenate(cols, axis=-1) * bg_ref[...].astype(F32)).astype(BF16)
    half = tm // 2
    for r in range(2):
        rs = slice(r * half, (r + 1) * half)
        ya = (u_scr[rs, :] * ag_ref[rs, :].astype(F32)).astype(BF16)
        merged = (m0_ref[rs, :].astype(F32) * _dot(ya, pa_ref[...])
                  + m1_ref[rs, :].astype(F32) * _dot(yb[rs], pb_ref[...])
                  + m2_ref[rs, :].astype(F32) * _dot(yc_ref[rs, :], pc_ref[...]))
        out_ref[rs, :] = x_ref[rs, :] + gate_ref[...] * _dot(merged.astype(BF16), wo_ref[...])


def merge_layer(x2, mods, o_groups, lse_groups, bg, yc, z2, lru_params, proj_a, proj_b, proj_c,
                w_o, *, layer, batch, tm):
    T, D = x2.shape
    tiles_per_batch = T // batch // tm
    rows = lambda width, col=0: pl.BlockSpec((tm, width), lambda i: (i, col))
    full = lambda a: pl.BlockSpec((None,) + a.shape[1:], lambda i: (layer,) + (0,) * (a.ndim - 1),
                                  pipeline_mode=pl.Buffered(1))

    def residue_major(g, width):
        dil = DIL_PATTERNS[g][1]
        assert tm % (dil * BF16_ROWS) == 0
        return pl.BlockSpec((None, dil, tm // dil, width),
                            lambda i: (i // tiles_per_batch, 0, i % tiles_per_batch, 0))

    weights = (proj_a, proj_b, proj_c, w_o)
    gate_spec = pl.BlockSpec((None, None, None, 1, D), lambda i: (layer, i // tiles_per_batch, 2, 0, 0))
    return pl.pallas_call(
        functools.partial(_merge_kernel, tiles_per_batch=tiles_per_batch),
        grid=(T // tm,),
        in_specs=[rows(D), gate_spec, rows(D, Z_AX), rows(D, Z_AG)]
                 + [residue_major(g, COL) for g in range(N_GROUPS)]
                 + [residue_major(g, LANES) for g in range(N_GROUPS)]
                 + [rows(COL), rows(D), rows(D, Z_MG), rows(D, Z_MG + 1), rows(D, Z_MG + 2)]
                 + [full(p) for p in lru_params] + [full(w) for w in weights],
        out_specs=rows(D),
        out_shape=jax.ShapeDtypeStruct((T, D), F32),
        scratch_shapes=[pltpu.VMEM((N_GROUPS, ATT_OUT // LANES, tm, LANES), F32),
                        pltpu.VMEM((N_GROUPS, tm, LANES), F32),
                        pltpu.VMEM((tm, D), F32), pltpu.VMEM((tm, D), F32),
                        pltpu.VMEM((SUBLANES, D), F32)],
        compiler_params=_params(("arbitrary",)),
        name="merge_layer",
    )(x2, mods, z2, z2, *o_groups, *lse_groups, bg, yc, z2, z2, z2, *lru_params, *weights)


def kernel(x, c, positions, ada_w, ada_b, norm_g, w_in, conv_w, conv_b, lru_wa, lru_ba, lru_wx,
           lru_bx, lru_lambda, qn_g, kn_g, gla_a1, gla_a2, gla_ab, gla_on_g, proj_a, proj_b,
           proj_c, w_o):
    B, S, D = x.shape
    T = B * S
    depth = ada_w.shape[0]
    tm_main = min(512, S)
    tm_att = min(512, S)
    tm_merge = min(512, S)
    tq_att = 512
    row = lambda v: v[:, None, :]
    w_main, w_att, gains = split_w_in(w_in, qn_g, kn_g)
    a1p = jnp.pad(gla_a1, ((0, 0), (0, 0), (0, LANES - GLA_RANK))).astype(BF16)
    a2p = jnp.pad(gla_a2, ((0, 0), (0, LANES - GLA_RANK), (0, 0))).astype(BF16)
    lru_params = (lru_block_diag(lru_wa), row(lru_ba), lru_block_diag(lru_wx), row(lru_bx),
                  row(lru_lambda))
    out_w = [w.astype(BF16) for w in (proj_a, proj_b, proj_c, w_o)]

    cos_t, sin_t = rope_tables(positions)
    mods = adaln_mod(c, ada_w, ada_b).reshape(depth, SUBLANES, 3, 1, D)
    x2 = x.reshape(T, D)
    for l in range(depth):
        z2, la, hb = proj_main(x2, mods, row(norm_g), w_main, a1p, a2p, row(gla_ab), conv_w,
                               row(conv_b), layer=l, batch=B, tm=tm_main)
        *qkv, bg, yc = proj_att(hb, w_att, cos_t, sin_t, gains, z2, la, row(gla_on_g), layer=l,
                                batch=B, tm=tm_att)
        outs = [attn_group(qkv[g], g, win, tq=tq_att) for g, (win, _) in enumerate(DIL_PATTERNS)]
        x2 = merge_layer(x2, mods, [o for o, _ in outs], [s for _, s in outs], bg, yc, z2,
                         lru_params, *out_w, layer=l, batch=B, tm=tm_merge)
    return x2.reshape(B, S, D)
```

```python
import functools

import jax
import jax.numpy as jnp
import numpy as np
from jax import lax
from jax.experimental import pallas as pl
from jax.experimental.pallas import tpu as pltpu

LRU_BLOCKS = 16
CONV_W = 4
LRU_C = 8.0
DIL_PATTERNS = ((128, 1), (512, 4), (2048, 16))
N_GROUPS = 3
HEADS_PER_GROUP = 8
HEAD_DIM = 64
HALF = HEAD_DIM // 2
ATT_OUT = HEADS_PER_GROUP * HEAD_DIM
BAND_BLK = 128
ROPE_THETA = 10000.0
GLA_HEADS = 4
GLA_RANK = 16
GLA_NORMALIZER = 16.0
GLA_CHUNK = 64
EPS = 1e-6

LANES = 128
SUBLANES = 8
BF16_ROWS = 16
MXU_DIM = 256
VMEM_LIMIT = 48 * 1024 * 1024
VMEM_LIMIT_MAIN = 56 * 1024 * 1024

COL = 512
N_ATT_TILES = 3 * N_GROUPS
NEG_BIG = -1e30
LOG2E = float(np.log2(np.e))
LN2 = float(np.log(2.0))
BF16 = jnp.bfloat16
F32 = jnp.float32


def _params(sem, vmem_limit=VMEM_LIMIT):
    return pltpu.CompilerParams(dimension_semantics=sem, vmem_limit_bytes=vmem_limit)


def _split_bf16(x):
    hi = x.astype(BF16)
    lo = (x - hi.astype(F32)).astype(BF16)
    return hi, lo


def _dot(a, b):
    return jnp.dot(a, b, preferred_element_type=F32)


def _dot_nt(a, b):
    return lax.dot_general(a, b, (((1,), (1,)), ((), ())), preferred_element_type=F32)


def _dot_tn(a, b):
    return lax.dot_general(a, b, (((0,), (0,)), ((), ())), preferred_element_type=F32)


def _sigmoid(x):
    return 0.5 * jnp.tanh(0.5 * x) + 0.5


def _silu(x):
    h = 0.5 * x
    return h + h * jnp.tanh(h)


def _softplus(x):
    return jnp.maximum(x, 0.0) + jnp.log1p(jnp.exp(-jnp.abs(x)))


def _pair_permute_cols(w):
    shp = w.shape
    w5 = w.reshape(shp[:-1] + (shp[-1] // LANES, 2, 2, HALF))
    return jnp.swapaxes(w5, -3, -2).reshape(shp)


def _rope_kernel(pos_ref, inv_ref, sgn_ref, cos_ref, sin_ref):
    ang = pos_ref[...].astype(F32) * inv_ref[...]
    cos_ref[...] = jnp.cos(ang)
    sin_ref[...] = jnp.sin(ang) * sgn_ref[...]


def rope_tables(positions):
    T = positions.size
    inv = ROPE_THETA ** (-(np.arange(HALF, dtype=np.float32) / HALF))
    lane = np.arange(LANES)
    inv_row = jnp.asarray(inv[lane % HALF].reshape(1, LANES), F32)
    sgn_row = jnp.asarray(np.where(lane < HEAD_DIM, -1.0, 1.0).reshape(1, LANES), F32)
    rows = min(T, 2048)
    return pl.pallas_call(
        _rope_kernel,
        grid=(T // rows,),
        in_specs=[pl.BlockSpec((rows, 1), lambda i: (i, 0)),
                  pl.BlockSpec((1, LANES), lambda i: (0, 0)),
                  pl.BlockSpec((1, LANES), lambda i: (0, 0))],
        out_specs=[pl.BlockSpec((rows, LANES), lambda i: (i, 0)),
                   pl.BlockSpec((rows, LANES), lambda i: (i, 0))],
        out_shape=[jax.ShapeDtypeStruct((T, LANES), F32)] * 2,
        compiler_params=_params(("arbitrary",)),
        name="rope_tables",
    )(positions.reshape(T, 1), inv_row, sgn_row)


def _mod_kernel(c_ref, w_ref, b_ref, o_ref):
    ca = _silu(c_ref[...])
    c_hi, c_lo = _split_bf16(ca)
    w_hi, w_lo = _split_bf16(w_ref[...])
    acc = _dot(c_hi, w_hi) + _dot(c_lo, w_hi) + _dot(c_hi, w_lo)
    o_ref[...] = acc + b_ref[...]


def adaln_mod(c, ada_w, ada_b):
    depth, D, D3 = ada_w.shape
    B = c.shape[0]
    cp = jnp.pad(c, ((0, SUBLANES - B), (0, 0)))
    tn = 1024
    return pl.pallas_call(
        _mod_kernel,
        grid=(depth, D3 // tn),
        in_specs=[pl.BlockSpec((SUBLANES, D), lambda l, j: (0, 0)),
                  pl.BlockSpec((None, D, tn), lambda l, j: (l, 0, j)),
                  pl.BlockSpec((None, 1, tn), lambda l, j: (l, 0, j))],
        out_specs=pl.BlockSpec((None, SUBLANES, tn), lambda l, j: (l, 0, j)),
        out_shape=jax.ShapeDtypeStruct((depth, SUBLANES, D3), F32),
        compiler_params=_params(("arbitrary", "arbitrary")),
        name="adaln_mod",
    )(cp, ada_w, ada_b.reshape(depth, 1, D3))


def _w_in_segments(D):
    att = N_GROUPS * ATT_OUT
    seg = {'a_x': (0, D), 'a_g': (D, D)}
    for g in range(N_GROUPS):
        seg[f'q{g}'] = (2 * D + g * ATT_OUT, ATT_OUT)
        seg[f'k{g}'] = (2 * D + att + g * ATT_OUT, ATT_OUT)
        seg[f'v{g}'] = (2 * D + 2 * att + g * ATT_OUT, ATT_OUT)
    off = 2 * D + 3 * att
    for name, width in (('b_g', ATT_OUT), ('c_q', COL), ('c_k', COL), ('c_v', D), ('c_g', D),
                        ('m_g', 3 * D)):
        seg[name] = (off, width)
        off += width
    return seg, off


Z_AX, Z_AG, Z_CV, Z_CG, Z_MG, Z_CQK = 0, 1, 2, 3, 4, 7
Z_TILES = 8


def _proj_main_kernel(x0_ref, xn_ref, ng_ref, scale0_ref, shift0_ref, scalen_ref, shiftn_ref,
                      w_ref, a1_ref, a2_ref, ab_ref, cw_ref, cb_ref,
                      z_ref, la_ref, hb_ref, h_scr, xe_scr, *, tiles_per_batch):
    i = pl.program_id(0)
    tm, D = hb_ref.shape

    def normed(x_ref, scale_ref, shift_ref):
        xf = x_ref[...]
        ms = jnp.mean(xf * xf, axis=-1, keepdims=True)
        y = xf * lax.rsqrt(ms + EPS) * ng_ref[...]
        return (y * (1.0 + scale_ref[...]) + shift_ref[...]).astype(BF16)

    cur = i % 2

    @pl.when(i == 0)
    def _():
        h_scr[0] = normed(x0_ref, scale0_ref, shift0_ref)

    first_in_batch = i % tiles_per_batch == 0

    @pl.when(first_in_batch)
    def _():
        xe_scr[0:SUBLANES, :] = jnp.zeros((SUBLANES, D), F32)

    @pl.when(jnp.logical_not(first_in_batch))
    def _():
        xe_scr[0:SUBLANES, :] = xe_scr[tm:tm + SUBLANES, :]

    hb = h_scr[cur]
    hb_ref[...] = hb
    t = _dot(hb, a1_ref[...]).astype(BF16)
    pre = _dot(t, a2_ref[...]) + ab_ref[...]
    la_ref[...] = -_softplus(-pre) * (1.0 / GLA_NORMALIZER)

    def tile(c):
        return _dot(hb, w_ref[:, c * D:(c + 1) * D])

    def put(c, val):
        z_ref[:, c * D:(c + 1) * D] = val.astype(z_ref.dtype)

    ax = tile(Z_AX)
    xe_scr[SUBLANES:SUBLANES + tm, :] = ax
    xc = cb_ref[...] + ax * cw_ref[CONV_W - 1:CONV_W, :]
    for k in range(CONV_W - 1):
        off = SUBLANES - (CONV_W - 1) + k
        xc = xc + xe_scr[off:off + tm, :] * cw_ref[k:k + 1, :]
    put(Z_AX, xc)
    put(Z_AG, _silu(tile(Z_AG)))
    put(Z_CV, tile(Z_CV))
    put(Z_CG, _silu(tile(Z_CG)))
    for c in range(3):
        put(Z_MG + c, _sigmoid(tile(Z_MG + c)))
    put(Z_CQK, tile(Z_CQK))
    h_scr[1 - cur] = normed(xn_ref, scalen_ref, shiftn_ref)


def proj_main(x2, mods, norm_g, w_main, a1p, a2p, gla_ab, conv_w, conv_b, *, layer, batch, tm):
    T, D = x2.shape
    kw = a2p.shape[-1]
    ntile = T // tm
    tiles_per_batch = ntile // batch
    nxt = lambda i: jnp.minimum(i + 1, ntile - 1)
    const = lambda shape: pl.BlockSpec(shape, lambda i: (0,) * len(shape),
                                       pipeline_mode=pl.Buffered(1))
    layer_const = lambda shape: pl.BlockSpec((None,) + shape, lambda i: (layer,) + (0,) * len(shape),
                                             pipeline_mode=pl.Buffered(1))
    mod_spec = lambda which, row: pl.BlockSpec((None, None, None, 1, D),
                                               lambda i: (layer, row(i), which, 0, 0))
    return pl.pallas_call(
        functools.partial(_proj_main_kernel, tiles_per_batch=tiles_per_batch),
        grid=(ntile,),
        in_specs=[pl.BlockSpec((tm, D), lambda i: (0, 0)),
                  pl.BlockSpec((tm, D), lambda i: (nxt(i), 0)),
                  layer_const((1, D)),
                  mod_spec(1, lambda i: 0), mod_spec(0, lambda i: 0),
                  mod_spec(1, lambda i: nxt(i) // tiles_per_batch),
                  mod_spec(0, lambda i: nxt(i) // tiles_per_batch),
                  layer_const((D, Z_TILES * D)),
                  layer_const((D, LANES)), layer_const((LANES, kw)), layer_const((1, kw)),
                  layer_const((CONV_W, D)), layer_const((1, D))],
        out_specs=[pl.BlockSpec((tm, Z_TILES * D), lambda i: (i, 0)),
                   pl.BlockSpec((tm, kw), lambda i: (i, 0)),
                   pl.BlockSpec((tm, D), lambda i: (i, 0))],
        out_shape=[jax.ShapeDtypeStruct((T, Z_TILES * D), BF16),
                   jax.ShapeDtypeStruct((T, kw), F32),
                   jax.ShapeDtypeStruct((T, D), BF16)],
        scratch_shapes=[pltpu.VMEM((2, tm, D), BF16), pltpu.VMEM((tm + SUBLANES, D), F32)],
        compiler_params=_params(("arbitrary",), VMEM_LIMIT_MAIN),
        name="proj_main",
    )(x2, x2, norm_g, mods, mods, mods, mods, w_main, a1p, a2p, gla_ab, conv_w, conv_b)


def _proj_att_kernel(h_ref, w_ref, cos_ref, sin_ref, gain_ref, ones_ref,
                     cq_ref, ck_ref, cv_ref, cg_ref, la_ref, og_ref,
                     qkv0_ref, qkv1_ref, qkv2_ref, bg_ref, yc_ref, val_scr, st_scr,
                     *, tiles_per_batch):
    tm = h_ref.shape[0]
    nchunk = COL // LANES

    @pl.when(pl.program_id(0) % tiles_per_batch == 0)
    def _():
        st_scr[...] = jnp.zeros_like(st_scr)

    _gla_tile(cq_ref, ck_ref, cv_ref, cg_ref, la_ref, og_ref, yc_ref, st_scr)

    h = h_ref[...]
    ones = ones_ref[...]
    cos = cos_ref[...]
    sin = sin_ref[...]
    bg_ref[...] = _silu(_dot(h, w_ref[:, N_ATT_TILES * COL:])).astype(bg_ref.dtype)
    for g, out_ref in enumerate((qkv0_ref, qkv1_ref, qkv2_ref)):
        dil = DIL_PATTERNS[g][1]
        for kind in range(3):
            t = 3 * g + kind
            acc = _dot(h, w_ref[:, t * COL:(t + 1) * COL])

            def emit(c, val, t=t, kind=kind, dil=dil, out_ref=out_ref):
                sl = slice(c * LANES, (c + 1) * LANES)
                if dil == 1:
                    out_ref[kind, 0, :, sl] = val.astype(out_ref.dtype)
                else:
                    slot = (t * nchunk + c) % val_scr.shape[0]
                    val_scr[slot] = val
                    for r in range(dil):
                        out_ref[kind, r, :, sl] = (
                            val_scr[slot, pl.ds(r, tm // dil, stride=dil), :].astype(out_ref.dtype))

            if kind == 2:
                for c in range(nchunk):
                    emit(c, acc[:, c * LANES:(c + 1) * LANES])
            else:
                qscale = HEAD_DIM ** -0.5 * LOG2E if kind == 0 else 1.0
                for c2 in range(COL // MXU_DIM):
                    sl2 = slice(c2 * MXU_DIM, (c2 + 1) * MXU_DIM)
                    a2 = acc[:, sl2]
                    ssq = _dot((a2 * a2).astype(BF16), ones)
                    xn = a2 * lax.rsqrt(ssq * (1.0 / HEAD_DIM) + EPS) * (gain_ref[t] * qscale)[:, sl2]
                    for cc in range(MXU_DIM // LANES):
                        xc = xn[:, cc * LANES:(cc + 1) * LANES]
                        emit(c2 * (MXU_DIM // LANES) + cc,
                             xc * cos + pltpu.roll(xc, HEAD_DIM, 1) * sin)


def _wprep_kernel(src_ref, perm_ref, w_ref, main_ref, att_ref, *, n_main):
    j = pl.program_id(1)
    permute = perm_ref[j] == 1

    @pl.when(j < n_main)
    def _():
        main_ref[...] = w_ref[...].astype(BF16)

    @pl.when(jnp.logical_and(j >= n_main, jnp.logical_not(permute)))
    def _():
        att_ref[...] = w_ref[...].astype(BF16)

    @pl.when(jnp.logical_and(j >= n_main, permute))
    def _():
        x = w_ref[...]
        lane = lax.broadcasted_iota(jnp.int32, x.shape, 1) % LANES
        y = jnp.where((lane >= HALF) & (lane < 2 * HALF), pltpu.roll(x, COL - HALF, 1),
                      jnp.where((lane >= 2 * HALF) & (lane < 3 * HALF), pltpu.roll(x, HALF, 1), x))
        att_ref[...] = y.astype(BF16)


def split_w_in(w_in, qn_g, kn_g):
    depth, D, n_in = w_in.shape
    seg, total = _w_in_segments(D)
    assert total == n_in and 2 * COL == D
    tiles = lambda name: [seg[name][0] // COL + k for k in range(seg[name][1] // COL)]
    src = sum((tiles(n) for n in ('a_x', 'a_g', 'c_v', 'c_g', 'm_g', 'c_q', 'c_k')), [])
    n_main = len(src)
    perm = [0] * n_main
    for g in range(N_GROUPS):
        src += tiles(f'q{g}') + tiles(f'k{g}') + tiles(f'v{g}')
        perm += [1, 1, 0]
    src += tiles('b_g')
    perm += [0]
    n_att = len(src) - n_main
    w_main, w_att = pl.pallas_call(
        functools.partial(_wprep_kernel, n_main=n_main),
        grid_spec=pltpu.PrefetchScalarGridSpec(
            num_scalar_prefetch=2,
            grid=(depth, len(src)),
            in_specs=[pl.BlockSpec((None, D, COL), lambda l, j, s, p: (l, 0, s[j]))],
            out_specs=[pl.BlockSpec((None, D, COL), lambda l, j, s, p: (l, 0, jnp.minimum(j, n_main - 1))),
                       pl.BlockSpec((None, D, COL), lambda l, j, s, p: (l, 0, jnp.maximum(j - n_main, 0)))]),
        out_shape=[jax.ShapeDtypeStruct((depth, D, n_main * COL), BF16),
                   jax.ShapeDtypeStruct((depth, D, n_att * COL), BF16)],
        compiler_params=_params(("arbitrary", "arbitrary")),
        name="weight_prep",
    )(jnp.asarray(src, jnp.int32), jnp.asarray(perm, jnp.int32), w_in)
    tiled = lambda g: _pair_permute_cols(jnp.tile(g, (1, 1, HEADS_PER_GROUP)))
    gains = jnp.stack([tiled(qn_g), tiled(kn_g), jnp.ones((depth, N_GROUPS, COL), F32)], axis=2)
    return w_main, w_att, gains.reshape(depth, N_ATT_TILES, 1, COL)


def proj_att(hb, w_att, cos_t, sin_t, gains, z2, la, gla_on_g, *, layer, batch, tm):
    T, D = hb.shape
    S = T // batch
    tiles_per_batch = S // tm
    ntile = N_ATT_TILES + 1
    kw = la.shape[-1]
    dv = gla_on_g.shape[-1]
    assert kw == COL and GLA_HEADS * dv == D and tm % GLA_CHUNK == 0
    rows = lambda width, col=0: pl.BlockSpec((tm, width), lambda i: (i, col))
    lane = np.arange(MXU_DIM)
    head = (lane // LANES) * 2 + (lane % HEAD_DIM) // HALF
    ones_bd = jnp.asarray(head[:, None] == head[None, :], BF16)

    def qkv_spec(g):
        dil = DIL_PATTERNS[g][1]
        assert tm % (dil * BF16_ROWS) == 0
        return pl.BlockSpec((None, 3, dil, tm // dil, COL),
                            lambda i: (i // tiles_per_batch, 0, 0, i % tiles_per_batch, 0))

    def qkv_shape(g):
        dil = DIL_PATTERNS[g][1]
        return jax.ShapeDtypeStruct((batch, 3, dil, S // dil, COL), BF16)

    const = lambda shape: pl.BlockSpec(shape, lambda i: (0,) * len(shape),
                                       pipeline_mode=pl.Buffered(1))
    layer_const = lambda shape: pl.BlockSpec((None,) + shape, lambda i: (layer,) + (0,) * len(shape),
                                             pipeline_mode=pl.Buffered(1))
    return pl.pallas_call(
        functools.partial(_proj_att_kernel, tiles_per_batch=tiles_per_batch),
        grid=(T // tm,),
        in_specs=[rows(D),
                  layer_const((D, ntile * COL)),
                  rows(LANES), rows(LANES),
                  layer_const((N_ATT_TILES, 1, COL)),
                  const((MXU_DIM, MXU_DIM)),
                  rows(kw, 2 * Z_CQK), rows(kw, 2 * Z_CQK + 1), rows(D, Z_CV), rows(D, Z_CG),
                  rows(kw), layer_const((1, dv))],
        out_specs=[qkv_spec(0), qkv_spec(1), qkv_spec(2), rows(COL), rows(D)],
        out_shape=[qkv_shape(0), qkv_shape(1), qkv_shape(2),
                   jax.ShapeDtypeStruct((T, COL), BF16), jax.ShapeDtypeStruct((T, D), BF16)],
        scratch_shapes=[pltpu.VMEM((2 * (COL // LANES), tm, LANES), F32),
                        pltpu.VMEM((GLA_HEADS, dv, kw // GLA_HEADS), F32)],
        compiler_params=_params(("arbitrary",)),
        name="proj_att",
    )(hb, w_att, cos_t, sin_t, gains, ones_bd, z2, z2, z2, z2, la, gla_on_g)


def _lru_tile(xc_ref, wa_ref, ba_ref, wx_ref, bx_ref, lam_ref, a_scr, u_scr, hc_scr):
    ts, W = xc_ref.shape
    neg_c_sp = -LRU_C * _softplus(-lam_ref[...])
    for g in range(W // MXU_DIM):
        sl = slice(g * MXU_DIM, (g + 1) * MXU_DIM)
        xb = xc_ref[:, sl]
        xg = xb.astype(F32)
        r = _sigmoid(_dot(xb, wa_ref[g]) + ba_ref[:, sl])
        i = _sigmoid(_dot(xb, wx_ref[g]) + bx_ref[:, sl])
        log_a = neg_c_sp[:, sl] * r
        a = jnp.exp(log_a)
        a_scr[:, sl] = a
        y = jnp.tanh(-log_a) * (1.0 + a * a)
        mult = jnp.where(y > 0.0, y * lax.rsqrt(y), 0.0)
        u_scr[:, sl] = mult * (i * xg)

    row = lax.broadcasted_iota(jnp.int32, (SUBLANES, W), 0)
    h_prev = hc_scr[...]
    for t in range(ts // SUBLANES):
        rs = slice(t * SUBLANES, (t + 1) * SUBLANES)
        a = a_scr[rs, :]
        u = u_scr[rs, :]
        d = 1
        while d < SUBLANES:
            keep = row >= d
            a_sh = jnp.where(keep, pltpu.roll(a, d, 0), 1.0)
            u_sh = jnp.where(keep, pltpu.roll(u, d, 0), 0.0)
            u = a * u_sh + u
            a = a * a_sh
            d *= 2
        h = a * h_prev + u
        u_scr[rs, :] = h
        h_prev = jnp.broadcast_to(h[SUBLANES - 1:SUBLANES, :], (SUBLANES, W))
    hc_scr[...] = h_prev


def lru_block_diag(w):
    depth, nblk, bw, _ = w.shape
    per = MXU_DIM // bw
    w5 = w.reshape(depth, nblk // per, per, bw, bw)
    eye = jnp.eye(per, dtype=w.dtype)
    return jnp.einsum('lgpjk,pq->lgpjqk', w5, eye).reshape(depth, nblk // per, MXU_DIM, MXU_DIM).astype(BF16)


def _attn_kernel(q_ref, kp_ref, kc_ref, vp_ref, vc_ref, o_ref, lse_ref, s_scr, p_scr, *, span, tq):
    n = pl.program_id(2)
    blk = BAND_BLK
    nqb = tq // blk
    npair = COL // LANES
    qi = lax.broadcasted_iota(jnp.int32, (blk, 2 * blk), 0)
    kj = lax.broadcasted_iota(jnp.int32, (blk, 2 * blk), 1)
    rel = qi + blk - kj
    band = (rel >= 0) & (rel <= span)
    bias = jnp.where(band, 0.0, NEG_BIG).astype(F32)
    bias_first = jnp.where(band & ((kj >= blk) | (n > 0)), 0.0, NEG_BIG).astype(F32)
    k = jnp.concatenate([kp_ref[...], kc_ref[...]], axis=0)
    v = jnp.concatenate([vp_ref[...], vc_ref[...]], axis=0)
    lane_k = lax.broadcasted_iota(jnp.int32, (blk + tq, LANES), 1)
    lane_q = lax.broadcasted_iota(jnp.int32, (tq, LANES), 1)
    lane = lax.broadcasted_iota(jnp.int32, (blk, LANES), 1)
    lanes_per_head = LANES // HEADS_PER_GROUP
    zero = jnp.zeros((), BF16)
    tile = lambda c, qb, hh: (c * nqb + qb) * 2 + hh

    for c in range(npair):
        sl = slice(c * LANES, (c + 1) * LANES)
        q_c = q_ref[:, sl]
        k_c = k[:, sl]
        q_heads = (jnp.where((lane_q % HEAD_DIM) < HALF, q_c, zero),
                   jnp.where((lane_q % HEAD_DIM) >= HALF, q_c, zero))
        for qb in range(nqb):
            for hh in range(2):
                s = _dot_nt(q_heads[hh][qb * blk:(qb + 1) * blk], k_c[qb * blk:(qb + 2) * blk])
                s_scr[tile(c, qb, hh)] = s + (bias_first if qb == 0 else bias)

    m_tiles = [jnp.zeros((blk, LANES), F32) for _ in range(nqb)]
    for c in range(npair):
        for qb in range(nqb):
            for hh in range(2):
                t = tile(c, qb, hh)
                s = s_scr[t]
                m = jnp.max(s, axis=-1, keepdims=True)
                p_scr[t] = jnp.exp2(s - m).astype(BF16)
                m_tiles[qb] = jnp.where(lane // lanes_per_head == 2 * c + hh, m, m_tiles[qb])

    l_tiles = [jnp.ones((blk, LANES), F32) for _ in range(nqb)]
    ones_k = jnp.ones((blk + tq, LANES), BF16)
    for c in range(npair):
        sl = slice(c * LANES, (c + 1) * LANES)
        v_c = v[:, sl]
        v_ext = (jnp.concatenate([jnp.where(lane_k < HEAD_DIM, v_c, zero), ones_k], axis=-1),
                 jnp.concatenate([jnp.where(lane_k >= HEAD_DIM, v_c, zero), ones_k], axis=-1))
        for qb in range(nqb):
            ks = slice(qb * blk, (qb + 2) * blk)
            oa = _dot(p_scr[tile(c, qb, 0)], v_ext[0][ks])
            ob = _dot(p_scr[tile(c, qb, 1)], v_ext[1][ks])
            la, lb = oa[:, LANES:], ob[:, LANES:]
            l_pair = jnp.where(lane < HEAD_DIM, la, lb)
            o_pair = oa[:, :LANES] + ob[:, :LANES]
            o_ref[qb * blk:(qb + 1) * blk, sl] = (o_pair / l_pair).astype(o_ref.dtype)
            l_tiles[qb] = jnp.where(lane // lanes_per_head == 2 * c, la,
                                    jnp.where(lane // lanes_per_head == 2 * c + 1, lb, l_tiles[qb]))
    for qb in range(nqb):
        lse_ref[qb * blk:(qb + 1) * blk, :] = m_tiles[qb] * LN2 + jnp.log(l_tiles[qb])


def attn_group(qkv, g, window, *, tq):
    B, _, dil, L, _ = qkv.shape
    tq = min(tq, L)
    assert L % tq == 0 and tq % BAND_BLK == 0
    per = tq // BAND_BLK
    cur = lambda kind: pl.BlockSpec((None, None, None, tq, COL), lambda b, r, n: (b, kind, r, n, 0))
    prev = lambda kind: pl.BlockSpec((None, None, None, BAND_BLK, COL),
                                     lambda b, r, n: (b, kind, r, jnp.maximum(n * per - 1, 0), 0))
    return pl.pallas_call(
        functools.partial(_attn_kernel, span=window // dil, tq=tq),
        grid=(B, dil, L // tq),
        in_specs=[cur(0), prev(1), cur(1), prev(2), cur(2)],
        out_specs=[pl.BlockSpec((None, None, tq, COL), lambda b, r, n: (b, r, n, 0)),
                   pl.BlockSpec((None, None, tq, LANES), lambda b, r, n: (b, r, n, 0))],
        out_shape=[jax.ShapeDtypeStruct((B, dil, L, COL), BF16),
                   jax.ShapeDtypeStruct((B, dil, L, LANES), F32)],
        scratch_shapes=[pltpu.VMEM((HEADS_PER_GROUP * per, BAND_BLK, 2 * BAND_BLK), F32),
                        pltpu.VMEM((HEADS_PER_GROUP * per, BAND_BLK, 2 * BAND_BLK), BF16)],
        compiler_params=_params(("arbitrary", "arbitrary", "arbitrary")),
        name=f"attn_group{g}",
    )(qkv, qkv, qkv, qkv, qkv)


def _gla_tile(q_ref, k_ref, v_ref, g_ref, la_ref, og_ref, y_ref, st_scr):
    C = GLA_CHUNK
    ts, kw = q_ref.shape
    dk = kw // GLA_HEADS
    dv = v_ref.shape[-1] // GLA_HEADS
    row = lax.broadcasted_iota(jnp.int32, (C, kw), 0)
    qi = lax.broadcasted_iota(jnp.int32, (C, C), 0)
    kj = lax.broadcasted_iota(jnp.int32, (C, C), 1)
    causal = qi >= kj
    for c in range(ts // C):
        rs = slice(c * C, (c + 1) * C)
        b = la_ref[rs, :]
        d = 1
        while d < C:
            b = b + jnp.where(row >= d, pltpu.roll(b, d, 0), 0.0)
            d *= 2
        b_last = b[C - 1:C, :]
        qf = q_ref[rs, :].astype(F32)
        kf = k_ref[rs, :].astype(F32)
        q_dec = (qf * (dk ** -0.5) * jnp.exp(b)).astype(BF16)
        k_inv = (kf * jnp.exp(-b)).astype(BF16)
        k_end = (kf * jnp.exp(b_last - b)).astype(BF16)
        dec = jnp.exp(b_last)
        for h in range(GLA_HEADS):
            ks = slice(h * dk, (h + 1) * dk)
            vs = slice(h * dv, (h + 1) * dv)
            vh = v_ref[rs, vs]
            att = jnp.where(causal, _dot_nt(q_dec[:, ks], k_inv[:, ks]), 0.0)
            st = st_scr[h]
            o = _dot(att.astype(BF16), vh) + _dot_nt(q_dec[:, ks], st.astype(BF16))
            st_scr[h] = st * dec[:, ks] + _dot_tn(vh, k_end[:, ks])
            ms = jnp.mean(o * o, axis=-1, keepdims=True)
            yn = o * lax.rsqrt(ms + EPS) * og_ref[...]
            y_ref[rs, vs] = (yn * g_ref[rs, vs].astype(F32)).astype(y_ref.dtype)


def _merge_kernel(x_ref, gate_ref, xc_ref, ag_ref, o0_ref, o1_ref, o2_ref, l0_ref, l1_ref, l2_ref,
                  bg_ref, yc_ref, m0_ref, m1_ref, m2_ref, wa_ref, ba_ref, wx_ref, bx_ref, lam_ref,
                  pa_ref, pb_ref, pc_ref, wo_ref, out_ref,
                  o_scr, l_scr, a_scr, u_scr, hc_scr, *, tiles_per_batch):
    tm = x_ref.shape[0]

    @pl.when(pl.program_id(0) % tiles_per_batch == 0)
    def _():
        hc_scr[...] = jnp.zeros_like(hc_scr)

    _lru_tile(xc_ref, wa_ref, ba_ref, wx_ref, bx_ref, lam_ref, a_scr, u_scr, hc_scr)

    for g, (o_ref, l_ref) in enumerate(((o0_ref, l0_ref), (o1_ref, l1_ref), (o2_ref, l2_ref))):
        dil = DIL_PATTERNS[g][1]
        for r in range(dil):
            rows = pl.ds(r, tm // dil, stride=dil) if dil > 1 else slice(None)
            l_scr[g, rows, :] = l_ref[r]
            for c in range(ATT_OUT // LANES):
                o_scr[g, c, rows, :] = o_ref[r, :, c * LANES:(c + 1) * LANES].astype(F32)
    l0, l1, l2 = l_scr[0], l_scr[1], l_scr[2]
    mx = jnp.maximum(jnp.maximum(l0, l1), l2)
    e0, e1, e2 = jnp.exp(l0 - mx), jnp.exp(l1 - mx), jnp.exp(l2 - mx)
    inv_den = 1.0 / (e0 + e1 + e2)
    wts = (e0 * inv_den, e1 * inv_den, e2 * inv_den)
    lanes_per_head = LANES // HEADS_PER_GROUP
    lane = lax.broadcasted_iota(jnp.int32, (tm, LANES), 1)
    upper = lane >= HEAD_DIM
    cols = []
    for c in range(ATT_OUT // LANES):
        acc = jnp.zeros((tm, LANES), F32)
        for g, w in enumerate(wts):
            h0 = 2 * c * lanes_per_head
            h1 = (2 * c + 1) * lanes_per_head
            wv = jnp.where(upper, w[:, h1:h1 + 1], w[:, h0:h0 + 1])
            acc = acc + wv * o_scr[g, c]
        cols.append(acc)
    yb = (jnp.concatenate(cols, axis=-1) * bg_ref[...].astype(F32)).astype(BF16)
    half = tm // 2
    for r in range(2):
        rs = slice(r * half, (r + 1) * half)
        ya = (u_scr[rs, :] * ag_ref[rs, :].astype(F32)).astype(BF16)
        merged = (m0_ref[rs, :].astype(F32) * _dot(ya, pa_ref[...])
                  + m1_ref[rs, :].astype(F32) * _dot(yb[rs], pb_ref[...])
                  + m2_ref[rs, :].astype(F32) * _dot(yc_ref[rs, :], pc_ref[...]))
        out_ref[rs, :] = x_ref[rs, :] + gate_ref[...] * _dot(merged.astype(BF16), wo_ref[...])


def merge_layer(x2, mods, o_groups, lse_groups, bg, yc, z2, lru_params, proj_a, proj_b, proj_c,
                w_o, *, layer, batch, tm):
    T, D = x2.shape
    tiles_per_batch = T // batch // tm
    rows = lambda width, col=0: pl.BlockSpec((tm, width), lambda i: (i, col))
    full = lambda a: pl.BlockSpec((None,) + a.shape[1:], lambda i: (layer,) + (0,) * (a.ndim - 1),
                                  pipeline_mode=pl.Buffered(1))

    def residue_major(g, width):
        dil = DIL_PATTERNS[g][1]
        assert tm % (dil * BF16_ROWS) == 0
        return pl.BlockSpec((None, dil, tm // dil, width),
                            lambda i: (i // tiles_per_batch, 0, i % tiles_per_batch, 0))

    weights = (proj_a, proj_b, proj_c, w_o)
    gate_spec = pl.BlockSpec((None, None, None, 1, D), lambda i: (layer, i // tiles_per_batch, 2, 0, 0))
    return pl.pallas_call(
        functools.partial(_merge_kernel, tiles_per_batch=tiles_per_batch),
        grid=(T // tm,),
        in_specs=[rows(D), gate_spec, rows(D, Z_AX), rows(D, Z_AG)]
                 + [residue_major(g, COL) for g in range(N_GROUPS)]
                 + [residue_major(g, LANES) for g in range(N_GROUPS)]
                 + [rows(COL), rows(D), rows(D, Z_MG), rows(D, Z_MG + 1), rows(D, Z_MG + 2)]
                 + [full(p) for p in lru_params] + [full(w) for w in weights],
        out_specs=rows(D),
        out_shape=jax.ShapeDtypeStruct((T, D), F32),
        scratch_shapes=[pltpu.VMEM((N_GROUPS, ATT_OUT // LANES, tm, LANES), F32),
                        pltpu.VMEM((N_GROUPS, tm, LANES), F32),
                        pltpu.VMEM((tm, D), F32), pltpu.VMEM((tm, D), F32),
                        pltpu.VMEM((SUBLANES, D), F32)],
        compiler_params=_params(("arbitrary",)),
        name="merge_layer",
    )(x2, mods, z2, z2, *o_groups, *lse_groups, bg, yc, z2, z2, z2, *lru_params, *weights)


def kernel(x, c, positions, ada_w, ada_b, norm_g, w_in, conv_w, conv_b, lru_wa, lru_ba, lru_wx,
           lru_bx, lru_lambda, qn_g, kn_g, gla_a1, gla_a2, gla_ab, gla_on_g, proj_a, proj_b,
           proj_c, w_o):
    B, S, D = x.shape
    T = B * S
    depth = ada_w.shape[0]
    tm_main = min(512, S)
    tm_att = min(512, S)
    tm_merge = min(512, S)
    tq_att = 1024
    row = lambda v: v[:, None, :]
    w_main, w_att, gains = split_w_in(w_in, qn_g, kn_g)
    a1p = jnp.pad(gla_a1, ((0, 0), (0, 0), (0, LANES - GLA_RANK))).astype(BF16)
    a2p = jnp.pad(gla_a2, ((0, 0), (0, LANES - GLA_RANK), (0, 0))).astype(BF16)
    lru_params = (lru_block_diag(lru_wa), row(lru_ba), lru_block_diag(lru_wx), row(lru_bx),
                  row(lru_lambda))
    out_w = [w.astype(BF16) for w in (proj_a, proj_b, proj_c, w_o)]

    cos_t, sin_t = rope_tables(positions)
    mods = adaln_mod(c, ada_w, ada_b).reshape(depth, SUBLANES, 3, 1, D)
    x2 = x.reshape(T, D)
    for l in range(depth):
        z2, la, hb = proj_main(x2, mods, row(norm_g), w_main, a1p, a2p, row(gla_ab), conv_w,
                               row(conv_b), layer=l, batch=B, tm=tm_main)
        *qkv, bg, yc = proj_att(hb, w_att, cos_t, sin_t, gains, z2, la, row(gla_on_g), layer=l,
                                batch=B, tm=tm_att)
        outs = [attn_group(qkv[g], g, win, tq=tq_att) for g, (win, _) in enumerate(DIL_PATTERNS)]
        x2 = merge_layer(x2, mods, [o for o, _ in outs], [s for _, s in outs], bg, yc, z2,
                         lru_params, *out_w, layer=l, batch=B, tm=tm_merge)
    return x2.reshape(B, S, D)
```

```python
import functools

import jax
import jax.numpy as jnp
import numpy as np
from jax import lax
from jax.experimental import pallas as pl
from jax.experimental.pallas import tpu as pltpu

LRU_BLOCKS = 16
CONV_W = 4
LRU_C = 8.0
DIL_PATTERNS = ((128, 1), (512, 4), (2048, 16))
N_GROUPS = 3
HEADS_PER_GROUP = 8
HEAD_DIM = 64
HALF = HEAD_DIM // 2
ATT_OUT = HEADS_PER_GROUP * HEAD_DIM
BAND_BLK = 128
ROPE_THETA = 10000.0
GLA_HEADS = 4
GLA_RANK = 16
GLA_NORMALIZER = 16.0
GLA_CHUNK = 64
EPS = 1e-6

LANES = 128
SUBLANES = 8
BF16_ROWS = 16
MXU_DIM = 256
VMEM_LIMIT = 48 * 1024 * 1024
VMEM_LIMIT_MAIN = 56 * 1024 * 1024

COL = 512
N_ATT_TILES = 3 * N_GROUPS
NEG_BIG = -1e30
LOG2E = float(np.log2(np.e))
LN2 = float(np.log(2.0))
BF16 = jnp.bfloat16
F32 = jnp.float32


def _params(sem, vmem_limit=VMEM_LIMIT):
    return pltpu.CompilerParams(dimension_semantics=sem, vmem_limit_bytes=vmem_limit)


def _split_bf16(x):
    hi = x.astype(BF16)
    lo = (x - hi.astype(F32)).astype(BF16)
    return hi, lo


def _dot(a, b):
    return jnp.dot(a, b, preferred_element_type=F32)


def _dot_nt(a, b):
    return lax.dot_general(a, b, (((1,), (1,)), ((), ())), preferred_element_type=F32)


def _dot_tn(a, b):
    return lax.dot_general(a, b, (((0,), (0,)), ((), ())), preferred_element_type=F32)


def _sigmoid(x):
    return 0.5 * jnp.tanh(0.5 * x) + 0.5


def _silu(x):
    h = 0.5 * x
    return h + h * jnp.tanh(h)


def _softplus(x):
    return jnp.maximum(x, 0.0) + jnp.log1p(jnp.exp(-jnp.abs(x)))


def _pair_permute_cols(w):
    shp = w.shape
    w5 = w.reshape(shp[:-1] + (shp[-1] // LANES, 2, 2, HALF))
    return jnp.swapaxes(w5, -3, -2).reshape(shp)


ROPE_PACK = LANES // HALF


def _rope_kernel(pos_ref, inv_ref, sgn_ref, cos_ref, sin_ref):
    rows = pos_ref.shape[0]
    lane = lax.broadcasted_iota(jnp.int32, (rows, LANES), 1)
    seg = lane // HALF
    pos = pos_ref[...].astype(F32)
    p = jnp.zeros((rows, LANES), F32)
    for q in range(ROPE_PACK):
        p = jnp.where(seg == q, pos[:, q:q + 1], p)
    ang = p * inv_ref[...]
    for packed, out_ref, sgn in ((jnp.cos(ang), cos_ref, None), (jnp.sin(ang), sin_ref, sgn_ref)):
        for q in range(ROPE_PACK):
            spread = packed
            for s in range(1, ROPE_PACK):
                spread = jnp.where(seg == (q + s) % ROPE_PACK, pltpu.roll(packed, s * HALF, 1), spread)
            if sgn is not None:
                spread = spread * sgn[...]
            out_ref[pl.ds(q, rows, stride=ROPE_PACK), :] = spread


def rope_tables(positions):
    T = positions.size
    inv = ROPE_THETA ** (-(np.arange(HALF, dtype=np.float32) / HALF))
    lane = np.arange(LANES)
    inv_row = jnp.asarray(inv[lane % HALF].reshape(1, LANES), F32)
    sgn_row = jnp.asarray(np.where(lane < HEAD_DIM, -1.0, 1.0).reshape(1, LANES), F32)
    rows = min(T, 2048)
    return pl.pallas_call(
        _rope_kernel,
        grid=(T // rows,),
        in_specs=[pl.BlockSpec((rows // ROPE_PACK, ROPE_PACK), lambda i: (i, 0)),
                  pl.BlockSpec((1, LANES), lambda i: (0, 0)),
                  pl.BlockSpec((1, LANES), lambda i: (0, 0))],
        out_specs=[pl.BlockSpec((rows, LANES), lambda i: (i, 0)),
                   pl.BlockSpec((rows, LANES), lambda i: (i, 0))],
        out_shape=[jax.ShapeDtypeStruct((T, LANES), F32)] * 2,
        compiler_params=_params(("arbitrary",)),
        name="rope_tables",
    )(positions.reshape(T // ROPE_PACK, ROPE_PACK), inv_row, sgn_row)


def _mod_kernel(c_ref, w_ref, b_ref, o_ref):
    ca = _silu(c_ref[...])
    c_hi, c_lo = _split_bf16(ca)
    w_hi, w_lo = _split_bf16(w_ref[...])
    acc = _dot(c_hi, w_hi) + _dot(c_lo, w_hi) + _dot(c_hi, w_lo)
    o_ref[...] = acc + b_ref[...]


def adaln_mod(c, ada_w, ada_b):
    depth, D, D3 = ada_w.shape
    B = c.shape[0]
    cp = jnp.pad(c, ((0, SUBLANES - B), (0, 0)))
    tn = 1024
    return pl.pallas_call(
        _mod_kernel,
        grid=(depth, D3 // tn),
        in_specs=[pl.BlockSpec((SUBLANES, D), lambda l, j: (0, 0)),
                  pl.BlockSpec((None, D, tn), lambda l, j: (l, 0, j)),
                  pl.BlockSpec((None, 1, tn), lambda l, j: (l, 0, j))],
        out_specs=pl.BlockSpec((None, SUBLANES, tn), lambda l, j: (l, 0, j)),
        out_shape=jax.ShapeDtypeStruct((depth, SUBLANES, D3), F32),
        compiler_params=_params(("arbitrary", "arbitrary")),
        name="adaln_mod",
    )(cp, ada_w, ada_b.reshape(depth, 1, D3))


def _w_in_segments(D):
    att = N_GROUPS * ATT_OUT
    seg = {'a_x': (0, D), 'a_g': (D, D)}
    for g in range(N_GROUPS):
        seg[f'q{g}'] = (2 * D + g * ATT_OUT, ATT_OUT)
        seg[f'k{g}'] = (2 * D + att + g * ATT_OUT, ATT_OUT)
        seg[f'v{g}'] = (2 * D + 2 * att + g * ATT_OUT, ATT_OUT)
    off = 2 * D + 3 * att
    for name, width in (('b_g', ATT_OUT), ('c_q', COL), ('c_k', COL), ('c_v', D), ('c_g', D),
                        ('m_g', 3 * D)):
        seg[name] = (off, width)
        off += width
    return seg, off


Z_AX, Z_AG, Z_CV, Z_CG, Z_MG, Z_CQK = 0, 1, 2, 3, 4, 7
Z_TILES = 8


def _proj_main_kernel(x0_ref, xn_ref, ng_ref, scale0_ref, shift0_ref, scalen_ref, shiftn_ref,
                      w_ref, a1_ref, a2_ref, ab_ref, cw_ref, cb_ref,
                      z_ref, la_ref, hb_ref, h_scr, xe_scr, *, tiles_per_batch):
    i = pl.program_id(0)
    tm, D = hb_ref.shape

    def normed(x_ref, scale_ref, shift_ref):
        xf = x_ref[...]
        ms = jnp.mean(xf * xf, axis=-1, keepdims=True)
        y = xf * lax.rsqrt(ms + EPS) * ng_ref[...]
        return (y * (1.0 + scale_ref[...]) + shift_ref[...]).astype(BF16)

    cur = i % 2

    @pl.when(i == 0)
    def _():
        h_scr[0] = normed(x0_ref, scale0_ref, shift0_ref)

    first_in_batch = i % tiles_per_batch == 0

    @pl.when(first_in_batch)
    def _():
        xe_scr[0:SUBLANES, :] = jnp.zeros((SUBLANES, D), F32)

    @pl.when(jnp.logical_not(first_in_batch))
    def _():
        xe_scr[0:SUBLANES, :] = xe_scr[tm:tm + SUBLANES, :]

    hb = h_scr[cur]
    hb_ref[...] = hb
    t = _dot(hb, a1_ref[...]).astype(BF16)
    pre = _dot(t, a2_ref[...]) + ab_ref[...]
    la_ref[...] = -_softplus(-pre) * (1.0 / GLA_NORMALIZER)

    def tile(c):
        return _dot(hb, w_ref[:, c * D:(c + 1) * D])

    def put(c, val):
        z_ref[:, c * D:(c + 1) * D] = val.astype(z_ref.dtype)

    ax = tile(Z_AX)
    xe_scr[SUBLANES:SUBLANES + tm, :] = ax
    xc = cb_ref[...] + ax * cw_ref[CONV_W - 1:CONV_W, :]
    for k in range(CONV_W - 1):
        off = SUBLANES - (CONV_W - 1) + k
        xc = xc + xe_scr[off:off + tm, :] * cw_ref[k:k + 1, :]
    put(Z_AX, xc)
    put(Z_AG, _silu(tile(Z_AG)))
    put(Z_CV, tile(Z_CV))
    put(Z_CG, _silu(tile(Z_CG)))
    for c in range(3):
        put(Z_MG + c, _sigmoid(tile(Z_MG + c)))
    put(Z_CQK, tile(Z_CQK))
    h_scr[1 - cur] = normed(xn_ref, scalen_ref, shiftn_ref)


def proj_main(x2, mods, norm_g, w_main, a1p, a2p, gla_ab, conv_w, conv_b, *, layer, batch, tm):
    T, D = x2.shape
    kw = a2p.shape[-1]
    ntile = T // tm
    tiles_per_batch = ntile // batch
    nxt = lambda i: jnp.minimum(i + 1, ntile - 1)
    const = lambda shape: pl.BlockSpec(shape, lambda i: (0,) * len(shape),
                                       pipeline_mode=pl.Buffered(1))
    layer_const = lambda shape: pl.BlockSpec((None,) + shape, lambda i: (layer,) + (0,) * len(shape),
                                             pipeline_mode=pl.Buffered(1))
    mod_spec = lambda which, row: pl.BlockSpec((None, None, None, 1, D),
                                               lambda i: (layer, row(i), which, 0, 0))
    return pl.pallas_call(
        functools.partial(_proj_main_kernel, tiles_per_batch=tiles_per_batch),
        grid=(ntile,),
        in_specs=[pl.BlockSpec((tm, D), lambda i: (0, 0)),
                  pl.BlockSpec((tm, D), lambda i: (nxt(i), 0)),
                  layer_const((1, D)),
                  mod_spec(1, lambda i: 0), mod_spec(0, lambda i: 0),
                  mod_spec(1, lambda i: nxt(i) // tiles_per_batch),
                  mod_spec(0, lambda i: nxt(i) // tiles_per_batch),
                  layer_const((D, Z_TILES * D)),
                  layer_const((D, LANES)), layer_const((LANES, kw)), layer_const((1, kw)),
                  layer_const((CONV_W, D)), layer_const((1, D))],
        out_specs=[pl.BlockSpec((tm, Z_TILES * D), lambda i: (i, 0)),
                   pl.BlockSpec((tm, kw), lambda i: (i, 0)),
                   pl.BlockSpec((tm, D), lambda i: (i, 0))],
        out_shape=[jax.ShapeDtypeStruct((T, Z_TILES * D), BF16),
                   jax.ShapeDtypeStruct((T, kw), F32),
                   jax.ShapeDtypeStruct((T, D), BF16)],
        scratch_shapes=[pltpu.VMEM((2, tm, D), BF16), pltpu.VMEM((tm + SUBLANES, D), F32)],
        compiler_params=_params(("arbitrary",), VMEM_LIMIT_MAIN),
        name="proj_main",
    )(x2, x2, norm_g, mods, mods, mods, mods, w_main, a1p, a2p, gla_ab, conv_w, conv_b)


def _proj_att_kernel(h_ref, w_ref, cos_ref, sin_ref, gain_ref, ones_ref,
                     cq_ref, ck_ref, cv_ref, cg_ref, la_ref, og_ref,
                     qkv0_ref, qkv1_ref, qkv2_ref, bg_ref, yc_ref, val_scr, st_scr,
                     *, tiles_per_batch):
    tm = h_ref.shape[0]
    nchunk = COL // LANES

    @pl.when(pl.program_id(0) % tiles_per_batch == 0)
    def _():
        st_scr[...] = jnp.zeros_like(st_scr)

    _gla_tile(cq_ref, ck_ref, cv_ref, cg_ref, la_ref, og_ref, yc_ref, st_scr)

    h = h_ref[...]
    ones = ones_ref[...]
    cos = cos_ref[...]
    sin = sin_ref[...]
    bg_ref[...] = _silu(_dot(h, w_ref[:, N_ATT_TILES * COL:])).astype(bg_ref.dtype)
    for g, out_ref in enumerate((qkv0_ref, qkv1_ref, qkv2_ref)):
        dil = DIL_PATTERNS[g][1]
        for kind in range(3):
            t = 3 * g + kind
            acc = _dot(h, w_ref[:, t * COL:(t + 1) * COL])

            def emit(c, val, t=t, kind=kind, dil=dil, out_ref=out_ref):
                sl = slice(c * LANES, (c + 1) * LANES)
                if dil == 1:
                    out_ref[kind, 0, :, sl] = val.astype(out_ref.dtype)
                else:
                    slot = (t * nchunk + c) % val_scr.shape[0]
                    val_scr[slot] = val
                    for r in range(dil):
                        out_ref[kind, r, :, sl] = (
                            val_scr[slot, pl.ds(r, tm // dil, stride=dil), :].astype(out_ref.dtype))

            if kind == 2:
                for c in range(nchunk):
                    emit(c, acc[:, c * LANES:(c + 1) * LANES])
            else:
                qscale = HEAD_DIM ** -0.5 * LOG2E if kind == 0 else 1.0
                for c2 in range(COL // MXU_DIM):
                    sl2 = slice(c2 * MXU_DIM, (c2 + 1) * MXU_DIM)
                    a2 = acc[:, sl2]
                    ssq = _dot((a2 * a2).astype(BF16), ones)
                    xn = a2 * lax.rsqrt(ssq * (1.0 / HEAD_DIM) + EPS) * (gain_ref[t] * qscale)[:, sl2]
                    for cc in range(MXU_DIM // LANES):
                        xc = xn[:, cc * LANES:(cc + 1) * LANES]
                        emit(c2 * (MXU_DIM // LANES) + cc,
                             xc * cos + pltpu.roll(xc, HEAD_DIM, 1) * sin)


def _wprep_kernel(src_ref, perm_ref, w_ref, main_ref, att_ref, *, n_main):
    j = pl.program_id(1)
    permute = perm_ref[j] == 1

    @pl.when(j < n_main)
    def _():
        main_ref[...] = w_ref[...].astype(BF16)

    @pl.when(jnp.logical_and(j >= n_main, jnp.logical_not(permute)))
    def _():
        att_ref[...] = w_ref[...].astype(BF16)

    @pl.when(jnp.logical_and(j >= n_main, permute))
    def _():
        x = w_ref[...]
        lane = lax.broadcasted_iota(jnp.int32, x.shape, 1) % LANES
        y = jnp.where((lane >= HALF) & (lane < 2 * HALF), pltpu.roll(x, COL - HALF, 1),
                      jnp.where((lane >= 2 * HALF) & (lane < 3 * HALF), pltpu.roll(x, HALF, 1), x))
        att_ref[...] = y.astype(BF16)


def split_w_in(w_in, qn_g, kn_g):
    depth, D, n_in = w_in.shape
    seg, total = _w_in_segments(D)
    assert total == n_in and 2 * COL == D
    tiles = lambda name: [seg[name][0] // COL + k for k in range(seg[name][1] // COL)]
    src = sum((tiles(n) for n in ('a_x', 'a_g', 'c_v', 'c_g', 'm_g', 'c_q', 'c_k')), [])
    n_main = len(src)
    perm = [0] * n_main
    for g in range(N_GROUPS):
        src += tiles(f'q{g}') + tiles(f'k{g}') + tiles(f'v{g}')
        perm += [1, 1, 0]
    src += tiles('b_g')
    perm += [0]
    n_att = len(src) - n_main
    w_main, w_att = pl.pallas_call(
        functools.partial(_wprep_kernel, n_main=n_main),
        grid_spec=pltpu.PrefetchScalarGridSpec(
            num_scalar_prefetch=2,
            grid=(depth, len(src)),
            in_specs=[pl.BlockSpec((None, D, COL), lambda l, j, s, p: (l, 0, s[j]))],
            out_specs=[pl.BlockSpec((None, D, COL), lambda l, j, s, p: (l, 0, jnp.minimum(j, n_main - 1))),
                       pl.BlockSpec((None, D, COL), lambda l, j, s, p: (l, 0, jnp.maximum(j - n_main, 0)))]),
        out_shape=[jax.ShapeDtypeStruct((depth, D, n_main * COL), BF16),
                   jax.ShapeDtypeStruct((depth, D, n_att * COL), BF16)],
        compiler_params=_params(("arbitrary", "arbitrary")),
        name="weight_prep",
    )(jnp.asarray(src, jnp.int32), jnp.asarray(perm, jnp.int32), w_in)
    tiled = lambda g: _pair_permute_cols(jnp.tile(g, (1, 1, HEADS_PER_GROUP)))
    gains = jnp.stack([tiled(qn_g), tiled(kn_g), jnp.ones((depth, N_GROUPS, COL), F32)], axis=2)
    return w_main, w_att, gains.reshape(depth, N_ATT_TILES, 1, COL)


def proj_att(hb, w_att, cos_t, sin_t, gains, z2, la, gla_on_g, *, layer, batch, tm):
    T, D = hb.shape
    S = T // batch
    tiles_per_batch = S // tm
    ntile = N_ATT_TILES + 1
    kw = la.shape[-1]
    dv = gla_on_g.shape[-1]
    assert kw == COL and GLA_HEADS * dv == D and tm % GLA_CHUNK == 0
    rows = lambda width, col=0: pl.BlockSpec((tm, width), lambda i: (i, col))
    lane = np.arange(MXU_DIM)
    head = (lane // LANES) * 2 + (lane % HEAD_DIM) // HALF
    ones_bd = jnp.asarray(head[:, None] == head[None, :], BF16)

    def qkv_spec(g):
        dil = DIL_PATTERNS[g][1]
        assert tm % (dil * BF16_ROWS) == 0
        return pl.BlockSpec((None, 3, dil, tm // dil, COL),
                            lambda i: (i // tiles_per_batch, 0, 0, i % tiles_per_batch, 0))

    def qkv_shape(g):
        dil = DIL_PATTERNS[g][1]
        return jax.ShapeDtypeStruct((batch, 3, dil, S // dil, COL), BF16)

    const = lambda shape: pl.BlockSpec(shape, lambda i: (0,) * len(shape),
                                       pipeline_mode=pl.Buffered(1))
    layer_const = lambda shape: pl.BlockSpec((None,) + shape, lambda i: (layer,) + (0,) * len(shape),
                                             pipeline_mode=pl.Buffered(1))
    return pl.pallas_call(
        functools.partial(_proj_att_kernel, tiles_per_batch=tiles_per_batch),
        grid=(T // tm,),
        in_specs=[rows(D),
                  layer_const((D, ntile * COL)),
                  rows(LANES), rows(LANES),
                  layer_const((N_ATT_TILES, 1, COL)),
                  const((MXU_DIM, MXU_DIM)),
                  rows(kw, 2 * Z_CQK), rows(kw, 2 * Z_CQK + 1), rows(D, Z_CV), rows(D, Z_CG),
                  rows(kw), layer_const((1, dv))],
        out_specs=[qkv_spec(0), qkv_spec(1), qkv_spec(2), rows(COL), rows(D)],
        out_shape=[qkv_shape(0), qkv_shape(1), qkv_shape(2),
                   jax.ShapeDtypeStruct((T, COL), BF16), jax.ShapeDtypeStruct((T, D), BF16)],
        scratch_shapes=[pltpu.VMEM((2 * (COL // LANES), tm, LANES), F32),
                        pltpu.VMEM((GLA_HEADS, dv, kw // GLA_HEADS), F32)],
        compiler_params=_params(("arbitrary",)),
        name="proj_att",
    )(hb, w_att, cos_t, sin_t, gains, ones_bd, z2, z2, z2, z2, la, gla_on_g)


def _lru_tile(xc_ref, wa_ref, ba_ref, wx_ref, bx_ref, lam_ref, a_scr, u_scr, hc_scr):
    ts, W = xc_ref.shape
    neg_c_sp = -LRU_C * _softplus(-lam_ref[...])
    for g in range(W // MXU_DIM):
        sl = slice(g * MXU_DIM, (g + 1) * MXU_DIM)
        xb = xc_ref[:, sl]
        xg = xb.astype(F32)
        r = _sigmoid(_dot(xb, wa_ref[g]) + ba_ref[:, sl])
        i = _sigmoid(_dot(xb, wx_ref[g]) + bx_ref[:, sl])
        log_a = neg_c_sp[:, sl] * r
        a = jnp.exp(log_a)
        a_scr[:, sl] = a
        y = jnp.tanh(-log_a) * (1.0 + a * a)
        mult = jnp.where(y > 0.0, y * lax.rsqrt(y), 0.0)
        u_scr[:, sl] = mult * (i * xg)

    row = lax.broadcasted_iota(jnp.int32, (SUBLANES, W), 0)
    h_prev = hc_scr[...]
    for t in range(ts // SUBLANES):
        rs = slice(t * SUBLANES, (t + 1) * SUBLANES)
        a = a_scr[rs, :]
        u = u_scr[rs, :]
        d = 1
        while d < SUBLANES:
            keep = row >= d
            a_sh = jnp.where(keep, pltpu.roll(a, d, 0), 1.0)
            u_sh = jnp.where(keep, pltpu.roll(u, d, 0), 0.0)
            u = a * u_sh + u
            a = a * a_sh
            d *= 2
        h = a * h_prev + u
        u_scr[rs, :] = h
        h_prev = jnp.broadcast_to(h[SUBLANES - 1:SUBLANES, :], (SUBLANES, W))
    hc_scr[...] = h_prev


def lru_block_diag(w):
    depth, nblk, bw, _ = w.shape
    per = MXU_DIM // bw
    w5 = w.reshape(depth, nblk // per, per, bw, bw)
    eye = jnp.eye(per, dtype=w.dtype)
    return jnp.einsum('lgpjk,pq->lgpjqk', w5, eye).reshape(depth, nblk // per, MXU_DIM, MXU_DIM).astype(BF16)


def _attn_kernel(q_ref, kp_ref, kc_ref, vp_ref, vc_ref, o_ref, lse_ref, s_scr, p_scr, *, span, tq):
    n = pl.program_id(2)
    blk = BAND_BLK
    nqb = tq // blk
    npair = COL // LANES
    qi = lax.broadcasted_iota(jnp.int32, (blk, 2 * blk), 0)
    kj = lax.broadcasted_iota(jnp.int32, (blk, 2 * blk), 1)
    rel = qi + blk - kj
    band = (rel >= 0) & (rel <= span)
    bias = jnp.where(band, 0.0, NEG_BIG).astype(F32)
    bias_first = jnp.where(band & ((kj >= blk) | (n > 0)), 0.0, NEG_BIG).astype(F32)
    k = jnp.concatenate([kp_ref[...], kc_ref[...]], axis=0)
    v = jnp.concatenate([vp_ref[...], vc_ref[...]], axis=0)
    lane_k = lax.broadcasted_iota(jnp.int32, (blk + tq, LANES), 1)
    lane_q = lax.broadcasted_iota(jnp.int32, (tq, LANES), 1)
    lane = lax.broadcasted_iota(jnp.int32, (blk, LANES), 1)
    lanes_per_head = LANES // HEADS_PER_GROUP
    zero = jnp.zeros((), BF16)
    tile = lambda c, qb, hh: (c * nqb + qb) * 2 + hh

    for c in range(npair):
        sl = slice(c * LANES, (c + 1) * LANES)
        q_c = q_ref[:, sl]
        k_c = k[:, sl]
        q_heads = (jnp.where((lane_q % HEAD_DIM) < HALF, q_c, zero),
                   jnp.where((lane_q % HEAD_DIM) >= HALF, q_c, zero))
        for qb in range(nqb):
            for hh in range(2):
                s = _dot_nt(q_heads[hh][qb * blk:(qb + 1) * blk], k_c[qb * blk:(qb + 2) * blk])
                s_scr[tile(c, qb, hh)] = s + (bias_first if qb == 0 else bias)

    m_tiles = [jnp.zeros((blk, LANES), F32) for _ in range(nqb)]
    for c in range(npair):
        for qb in range(nqb):
            for hh in range(2):
                t = tile(c, qb, hh)
                s = s_scr[t]
                m = jnp.max(s, axis=-1, keepdims=True)
                p_scr[t] = jnp.exp2(s - m).astype(BF16)
                m_tiles[qb] = jnp.where(lane // lanes_per_head == 2 * c + hh, m, m_tiles[qb])

    l_tiles = [jnp.ones((blk, LANES), F32) for _ in range(nqb)]
    ones_k = jnp.ones((blk + tq, LANES), BF16)
    for c in range(npair):
        sl = slice(c * LANES, (c + 1) * LANES)
        v_c = v[:, sl]
        v_ext = (jnp.concatenate([jnp.where(lane_k < HEAD_DIM, v_c, zero), ones_k], axis=-1),
                 jnp.concatenate([jnp.where(lane_k >= HEAD_DIM, v_c, zero), ones_k], axis=-1))
        for qb in range(nqb):
            ks = slice(qb * blk, (qb + 2) * blk)
            oa = _dot(p_scr[tile(c, qb, 0)], v_ext[0][ks])
            ob = _dot(p_scr[tile(c, qb, 1)], v_ext[1][ks])
            la, lb = oa[:, LANES:], ob[:, LANES:]
            l_pair = jnp.where(lane < HEAD_DIM, la, lb)
            o_pair = oa[:, :LANES] + ob[:, :LANES]
            o_ref[qb * blk:(qb + 1) * blk, sl] = (o_pair / l_pair).astype(o_ref.dtype)
            l_tiles[qb] = jnp.where(lane // lanes_per_head == 2 * c, la,
                                    jnp.where(lane // lanes_per_head == 2 * c + 1, lb, l_tiles[qb]))
    for qb in range(nqb):
        lse_ref[qb * blk:(qb + 1) * blk, :] = m_tiles[qb] * LN2 + jnp.log(l_tiles[qb])


def attn_group(qkv, g, window, *, tq):
    B, _, dil, L, _ = qkv.shape
    tq = min(tq, L)
    assert L % tq == 0 and tq % BAND_BLK == 0
    per = tq // BAND_BLK
    cur = lambda kind: pl.BlockSpec((None, None, None, tq, COL), lambda b, r, n: (b, kind, r, n, 0))
    prev = lambda kind: pl.BlockSpec((None, None, None, BAND_BLK, COL),
                                     lambda b, r, n: (b, kind, r, jnp.maximum(n * per - 1, 0), 0))
    return pl.pallas_call(
        functools.partial(_attn_kernel, span=window // dil, tq=tq),
        grid=(B, dil, L // tq),
        in_specs=[cur(0), prev(1), cur(1), prev(2), cur(2)],
        out_specs=[pl.BlockSpec((None, None, tq, COL), lambda b, r, n: (b, r, n, 0)),
                   pl.BlockSpec((None, None, tq, LANES), lambda b, r, n: (b, r, n, 0))],
        out_shape=[jax.ShapeDtypeStruct((B, dil, L, COL), BF16),
                   jax.ShapeDtypeStruct((B, dil, L, LANES), F32)],
        scratch_shapes=[pltpu.VMEM((HEADS_PER_GROUP * per, BAND_BLK, 2 * BAND_BLK), F32),
                        pltpu.VMEM((HEADS_PER_GROUP * per, BAND_BLK, 2 * BAND_BLK), BF16)],
        compiler_params=_params(("arbitrary", "arbitrary", "arbitrary")),
        name=f"attn_group{g}",
    )(qkv, qkv, qkv, qkv, qkv)


def _gla_tile(q_ref, k_ref, v_ref, g_ref, la_ref, og_ref, y_ref, st_scr):
    C = GLA_CHUNK
    ts, kw = q_ref.shape
    dk = kw // GLA_HEADS
    dv = v_ref.shape[-1] // GLA_HEADS
    row = lax.broadcasted_iota(jnp.int32, (C, kw), 0)
    qi = lax.broadcasted_iota(jnp.int32, (C, C), 0)
    kj = lax.broadcasted_iota(jnp.int32, (C, C), 1)
    causal = qi >= kj
    for c in range(ts // C):
        rs = slice(c * C, (c + 1) * C)
        b = la_ref[rs, :]
        d = 1
        while d < C:
            b = b + jnp.where(row >= d, pltpu.roll(b, d, 0), 0.0)
            d *= 2
        b_last = b[C - 1:C, :]
        qf = q_ref[rs, :].astype(F32)
        kf = k_ref[rs, :].astype(F32)
        q_dec = (qf * (dk ** -0.5) * jnp.exp(b)).astype(BF16)
        k_inv = (kf * jnp.exp(-b)).astype(BF16)
        k_end = (kf * jnp.exp(b_last - b)).astype(BF16)
        dec = jnp.exp(b_last)
        for h in range(GLA_HEADS):
            ks = slice(h * dk, (h + 1) * dk)
            vs = slice(h * dv, (h + 1) * dv)
            vh = v_ref[rs, vs]
            att = jnp.where(causal, _dot_nt(q_dec[:, ks], k_inv[:, ks]), 0.0)
            st = st_scr[h]
            o = _dot(att.astype(BF16), vh) + _dot_nt(q_dec[:, ks], st.astype(BF16))
            st_scr[h] = st * dec[:, ks] + _dot_tn(vh, k_end[:, ks])
            ms = jnp.mean(o * o, axis=-1, keepdims=True)
            yn = o * lax.rsqrt(ms + EPS) * og_ref[...]
            y_ref[rs, vs] = (yn * g_ref[rs, vs].astype(F32)).astype(y_ref.dtype)


def _merge_kernel(x_ref, gate_ref, xc_ref, ag_ref, o0_ref, o1_ref, o2_ref, l0_ref, l1_ref, l2_ref,
                  bg_ref, yc_ref, m0_ref, m1_ref, m2_ref, wa_ref, ba_ref, wx_ref, bx_ref, lam_ref,
                  pa_ref, pb_ref, pc_ref, wo_ref, out_ref,
                  o_scr, l_scr, a_scr, u_scr, hc_scr, *, tiles_per_batch):
    tm = x_ref.shape[0]

    @pl.when(pl.program_id(0) % tiles_per_batch == 0)
    def _():
        hc_scr[...] = jnp.zeros_like(hc_scr)

    _lru_tile(xc_ref, wa_ref, ba_ref, wx_ref, bx_ref, lam_ref, a_scr, u_scr, hc_scr)

    for g, (o_ref, l_ref) in enumerate(((o0_ref, l0_ref), (o1_ref, l1_ref), (o2_ref, l2_ref))):
        dil = DIL_PATTERNS[g][1]
        for r in range(dil):
            rows = pl.ds(r, tm // dil, stride=dil) if dil > 1 else slice(None)
            l_scr[g, rows, :] = l_ref[r]
            for c in range(ATT_OUT // LANES):
                o_scr[g, c, rows, :] = o_ref[r, :, c * LANES:(c + 1) * LANES].astype(F32)
    l0, l1, l2 = l_scr[0], l_scr[1], l_scr[2]
    mx = jnp.maximum(jnp.maximum(l0, l1), l2)
    e0, e1, e2 = jnp.exp(l0 - mx), jnp.exp(l1 - mx), jnp.exp(l2 - mx)
    inv_den = 1.0 / (e0 + e1 + e2)
    wts = (e0 * inv_den, e1 * inv_den, e2 * inv_den)
    lanes_per_head = LANES // HEADS_PER_GROUP
    lane = lax.broadcasted_iota(jnp.int32, (tm, LANES), 1)
    upper = lane >= HEAD_DIM
    cols = []
    for c in range(ATT_OUT // LANES):
        acc = jnp.zeros((tm, LANES), F32)
        for g, w in enumerate(wts):
            h0 = 2 * c * lanes_per_head
            h1 = (2 * c + 1) * lanes_per_head
            wv = jnp.where(upper, w[:, h1:h1 + 1], w[:, h0:h0 + 1])
            acc = acc + wv * o_scr[g, c]
        cols.append(acc)
    yb = (jnp.concatenate(cols, axis=-1) * bg_ref[...].astype(F32)).astype(BF16)
    half = tm // 2
    for r in range(2):
        rs = slice(r * half, (r + 1) * half)
        ya = (u_scr[rs, :] * ag_ref[rs, :].astype(F32)).astype(BF16)
        merged = (m0_ref[rs, :].astype(F32) * _dot(ya, pa_ref[...])
                  + m1_ref[rs, :].astype(F32) * _dot(yb[rs], pb_ref[...])
                  + m2_ref[rs, :].astype(F32) * _dot(yc_ref[rs, :], pc_ref[...]))
        out_ref[rs, :] = x_ref[rs, :] + gate_ref[...] * _dot(merged.astype(BF16), wo_ref[...])


def merge_layer(x2, mods, o_groups, lse_groups, bg, yc, z2, lru_params, proj_a, proj_b, proj_c,
                w_o, *, layer, batch, tm):
    T, D = x2.shape
    tiles_per_batch = T // batch // tm
    rows = lambda width, col=0: pl.BlockSpec((tm, width), lambda i: (i, col))
    full = lambda a: pl.BlockSpec((None,) + a.shape[1:], lambda i: (layer,) + (0,) * (a.ndim - 1),
                                  pipeline_mode=pl.Buffered(1))

    def residue_major(g, width):
        dil = DIL_PATTERNS[g][1]
        assert tm % (dil * BF16_ROWS) == 0
        return pl.BlockSpec((None, dil, tm // dil, width),
                            lambda i: (i // tiles_per_batch, 0, i % tiles_per_batch, 0))

    weights = (proj_a, proj_b, proj_c, w_o)
    gate_spec = pl.BlockSpec((None, None, None, 1, D), lambda i: (layer, i // tiles_per_batch, 2, 0, 0))
    return pl.pallas_call(
        functools.partial(_merge_kernel, tiles_per_batch=tiles_per_batch),
        grid=(T // tm,),
        in_specs=[rows(D), gate_spec, rows(D, Z_AX), rows(D, Z_AG)]
                 + [residue_major(g, COL) for g in range(N_GROUPS)]
                 + [residue_major(g, LANES) for g in range(N_GROUPS)]
                 + [rows(COL), rows(D), rows(D, Z_MG), rows(D, Z_MG + 1), rows(D, Z_MG + 2)]
                 + [full(p) for p in lru_params] + [full(w) for w in weights],
        out_specs=rows(D),
        out_shape=jax.ShapeDtypeStruct((T, D), F32),
        scratch_shapes=[pltpu.VMEM((N_GROUPS, ATT_OUT // LANES, tm, LANES), F32),
                        pltpu.VMEM((N_GROUPS, tm, LANES), F32),
                        pltpu.VMEM((tm, D), F32), pltpu.VMEM((tm, D), F32),
                        pltpu.VMEM((SUBLANES, D), F32)],
        compiler_params=_params(("arbitrary",)),
        name="merge_layer",
    )(x2, mods, z2, z2, *o_groups, *lse_groups, bg, yc, z2, z2, z2, *lru_params, *weights)


def kernel(x, c, positions, ada_w, ada_b, norm_g, w_in, conv_w, conv_b, lru_wa, lru_ba, lru_wx,
           lru_bx, lru_lambda, qn_g, kn_g, gla_a1, gla_a2, gla_ab, gla_on_g, proj_a, proj_b,
           proj_c, w_o):
    B, S, D = x.shape
    T = B * S
    depth = ada_w.shape[0]
    tm_main = min(512, S)
    tm_att = min(512, S)
    tm_merge = min(512, S)
    tq_att = 2048
    row = lambda v: v[:, None, :]
    w_main, w_att, gains = split_w_in(w_in, qn_g, kn_g)
    a1p = jnp.pad(gla_a1, ((0, 0), (0, 0), (0, LANES - GLA_RANK))).astype(BF16)
    a2p = jnp.pad(gla_a2, ((0, 0), (0, LANES - GLA_RANK), (0, 0))).astype(BF16)
    lru_params = (lru_block_diag(lru_wa), row(lru_ba), lru_block_diag(lru_wx), row(lru_bx),
                  row(lru_lambda))
    out_w = [w.astype(BF16) for w in (proj_a, proj_b, proj_c, w_o)]

    cos_t, sin_t = rope_tables(positions)
    mods = adaln_mod(c, ada_w, ada_b).reshape(depth, SUBLANES, 3, 1, D)
    x2 = x.reshape(T, D)
    for l in range(depth):
        z2, la, hb = proj_main(x2, mods, row(norm_g), w_main, a1p, a2p, row(gla_ab), conv_w,
                               row(conv_b), layer=l, batch=B, tm=tm_main)
        *qkv, bg, yc = proj_att(hb, w_att, cos_t, sin_t, gains, z2, la, row(gla_on_g), layer=l,
                                batch=B, tm=tm_att)
        outs = [attn_group(qkv[g], g, win, tq=tq_att) for g, (win, _) in enumerate(DIL_PATTERNS)]
        x2 = merge_layer(x2, mods, [o for o, _ in outs], [s for _, s in outs], bg, yc, z2,
                         lru_params, *out_w, layer=l, batch=B, tm=tm_merge)
    return x2.reshape(B, S, D)
```
